```python
import jax, jax.numpy as jnp
from jax import lax
import numpy as np

D_MODEL = 1024
BATCH = 4
SEQ = 4096
DEPTH = 4

PLE_DIM = 256
D_MIX = D_MODEL
W_GRP = D_MIX // 4
N_HEADS_GRP = 4
HEAD_DIM = W_GRP // N_HEADS_GRP
GMLP_CHUNK = 128
RGLRU_CONV = 4
RGLRU_C = 8.0
HGRN_CHUNK = 64
POOL_WINDOWS = (2, 4, 8, 16)
D_FF = 2816
FFN_CONV = 3
EPS = 1e-6
COLS_A = 2 * W_GRP
COLS_B = 2 * W_GRP
COLS_C = 4 * W_GRP
COLS_D = W_GRP
OFF_B = COLS_A
OFF_C = OFF_B + COLS_B
OFF_D = OFF_C + COLS_C
D_PROJ = OFF_D + COLS_D

kernel_name = "hymba_style_gmlp_rglru_hgrn2_pool_hybrid"


def rms_norm(x, g):
    xf = x.astype(jnp.float32)
    y = xf * lax.rsqrt(jnp.mean(xf * xf, axis=-1, keepdims=True) + EPS)
    return (y * g.astype(jnp.float32)).astype(x.dtype)


def causal_dwconv(x, w, b):
    k_width = w.shape[0]
    s = x.shape[1]
    xp = jnp.pad(x, ((0, 0), (k_width - 1, 0), (0, 0)))
    y = b
    for k in range(k_width):
        y = y + xp[:, k:k + s] * w[k]
    return y


def gmlp_mixer(ab, ln_g, ln_b, ws, bs):
    bsz, s, _ = ab.shape
    ab = jax.nn.gelu(ab)
    u, v = jnp.split(ab, 2, axis=-1)
    vf = v.astype(jnp.float32)
    mu = jnp.mean(vf, axis=-1, keepdims=True)
    var = jnp.mean(jnp.square(vf - mu), axis=-1, keepdims=True)
    vn = ((vf - mu) * lax.rsqrt(var + EPS) * ln_g.astype(jnp.float32) + ln_b.astype(jnp.float32)).astype(v.dtype)
    vn = vn.reshape(bsz, s // GMLP_CHUNK, GMLP_CHUNK, N_HEADS_GRP, HEAD_DIM)
    mask = jnp.tril(jnp.ones((GMLP_CHUNK, GMLP_CHUNK), dtype=bool))
    wm = jnp.where(mask, ws, jnp.zeros_like(ws))
    sv = jnp.einsum('hts,bnshd->bnthd', wm, vn) + bs.T[:, :, None]
    return u * sv.reshape(bsz, s, W_GRP)


def rglru_mixer(xb, gb, conv_w, conv_b, wa, ba, wx, bx, lam):
    bsz, s, _ = xb.shape
    xc = causal_dwconv(xb, conv_w, conv_b)
    xh = xc.reshape(bsz, s, N_HEADS_GRP, HEAD_DIM)
    r = jax.nn.sigmoid(jnp.einsum('bshd,hde->bshe', xh, wa).reshape(bsz, s, W_GRP) + ba)
    i = jax.nn.sigmoid(jnp.einsum('bshd,hde->bshe', xh, wx).reshape(bsz, s, W_GRP) + bx)
    log_a = -RGLRU_C * r.astype(jnp.float32) * jax.nn.softplus(-lam.astype(jnp.float32))
    a = jnp.exp(log_a)
    mult = jnp.sqrt(-jnp.expm1(2.0 * log_a))
    bterm = mult * (i * xc).astype(jnp.float32)

    def combine(c1, c2):
        a1, b1 = c1
        a2, b2 = c2
        return a1 * a2, a2 * b1 + b2

    _, h = lax.associative_scan(combine, (a, bterm), axis=1)
    return h.astype(xb.dtype) * jax.nn.gelu(gb)


def hgrn2_mixer(q, f, i, g, lb, norm_g):
    bsz, s, _ = q.shape
    n_chunks = s // HGRN_CHUNK
    qf = jax.nn.silu(q.astype(jnp.float32))
    fgate = lb + (1.0 - lb) * jax.nn.sigmoid(f.astype(jnp.float32))
    log_f = jnp.log(fgate)
    kf = 1.0 - fgate
    vf = i.astype(jnp.float32)

    def to_chunks(t):
        return t.reshape(bsz, n_chunks, HGRN_CHUNK, N_HEADS_GRP, HEAD_DIM).transpose(1, 0, 3, 2, 4)

    qc, kc, vc = to_chunks(qf), to_chunks(kf), to_chunks(vf)
    bc = jnp.cumsum(to_chunks(log_f), axis=3)
    mask = jnp.tril(jnp.ones((HGRN_CHUNK, HGRN_CHUNK), dtype=bool))[:, :, None]

    def step(state, xs):
        qq, kk, vv, bb = xs
        diff = bb[:, :, :, None, :] - bb[:, :, None, :, :]
        decay = jnp.exp(jnp.where(mask, diff, -jnp.inf))
        att = jnp.einsum('bhtd,bhsd,bhtsd->bhts', qq, kk, decay)
        o = jnp.einsum('bhts,bhsv->bhtv', att, vv) + jnp.einsum('bhtd,bhdv->bhtv', qq * jnp.exp(bb), state)
        bl = bb[:, :, -1:, :]
        new_state = jnp.exp(bl[:, :, 0, :])[..., None] * state + jnp.einsum('bhsd,bhsv->bhdv', kk * jnp.exp(bl - bb), vv)
        return new_state, o

    s0 = jnp.zeros((bsz, N_HEADS_GRP, HEAD_DIM, HEAD_DIM), jnp.float32)
    _, o = lax.scan(step, s0, (qc, kc, vc, bc))
    o = o.transpose(1, 0, 3, 2, 4).reshape(bsz, s, N_HEADS_GRP, HEAD_DIM)
    o = o * lax.rsqrt(jnp.mean(o * o, axis=-1, keepdims=True) + EPS) * norm_g.astype(jnp.float32)
    o = o.reshape(bsz, s, W_GRP) * jax.nn.silu(g.astype(jnp.float32))
    return o.astype(q.dtype)


def pool_mixer(xd, wd, scale):
    bsz, s, _ = xd.shape
    xf = xd.astype(jnp.float32)
    cs = jnp.cumsum(xf, axis=1)
    pos = jnp.arange(1, s + 1, dtype=jnp.float32)[None, :, None]
    outs = []
    for j, w in enumerate(POOL_WINDOWS):
        c = cs[..., j * HEAD_DIM:(j + 1) * HEAD_DIM]
        shifted = jnp.pad(c, ((0, 0), (w, 0), (0, 0)))[:, :s]
        mean = (c - shifted) / jnp.minimum(pos, float(w))
        outs.append(mean - xf[..., j * HEAD_DIM:(j + 1) * HEAD_DIM])
    pooled = jnp.stack(outs, axis=2)
    y = jnp.einsum('bsgd,gde->bsge', pooled, wd.astype(jnp.float32)).reshape(bsz, s, W_GRP)
    return (y * scale.astype(jnp.float32)).astype(xd.dtype)


def setup_inputs(seed: int = 0) -> dict:
    key = jax.random.key(seed)
    ks = jax.random.split(key, 32)

    def nrm(k, shape, scale):
        return jax.random.normal(k, shape, jnp.float32) * scale

    u = jax.random.uniform(ks[14], (DEPTH, W_GRP), jnp.float32, 0.9, 0.999)
    a_base = u ** (1.0 / RGLRU_C)
    b_lam = jnp.log(a_base) - jnp.log1p(-a_base)
    return {
        "x": nrm(ks[0], (BATCH, SEQ, D_MODEL), 1.0),
        "p": nrm(ks[1], (DEPTH, BATCH, SEQ, PLE_DIM), 1.0),
        "norm1_g": 1.0 + nrm(ks[2], (DEPTH, D_MODEL), 0.02),
        "w_in": nrm(ks[3], (DEPTH, D_MODEL, D_PROJ), D_MODEL ** -0.5),
        "a_ln_g": 1.0 + nrm(ks[4], (DEPTH, W_GRP), 0.02),
        "a_ln_b": nrm(ks[5], (DEPTH, W_GRP), 0.02),
        "a_ws": nrm(ks[6], (DEPTH, N_HEADS_GRP, GMLP_CHUNK, GMLP_CHUNK), GMLP_CHUNK ** -0.5),
        "a_bs": 1.0 + nrm(ks[7], (DEPTH, N_HEADS_GRP, GMLP_CHUNK), 0.1),
        "b_conv_w": nrm(ks[8], (DEPTH, RGLRU_CONV, W_GRP), RGLRU_CONV ** -0.5),
        "b_conv_b": nrm(ks[9], (DEPTH, W_GRP), 0.02),
        "b_wa": nrm(ks[10], (DEPTH, N_HEADS_GRP, HEAD_DIM, HEAD_DIM), HEAD_DIM ** -0.5),
        "b_ba": nrm(ks[11], (DEPTH, W_GRP), 0.02),
        "b_wx": nrm(ks[12], (DEPTH, N_HEADS_GRP, HEAD_DIM, HEAD_DIM), HEAD_DIM ** -0.5),
        "b_bx": nrm(ks[13], (DEPTH, W_GRP), 0.02),
        "b_lam": b_lam,
        "c_lb": nrm(ks[15], (DEPTH, W_GRP), 0.5),
        "c_norm_g": 1.0 + nrm(ks[16], (DEPTH, HEAD_DIM), 0.02),
        "d_w": nrm(ks[17], (DEPTH, N_HEADS_GRP, HEAD_DIM, HEAD_DIM), HEAD_DIM ** -0.5),
        "d_scale": 1.0 + nrm(ks[18], (DEPTH, W_GRP), 0.1),
        "w_out": nrm(ks[19], (DEPTH, D_MIX, D_MODEL), D_MIX ** -0.5),
        "norm2_g": 1.0 + nrm(ks[20], (DEPTH, D_MODEL), 0.02),
        "w_up": nrm(ks[21], (DEPTH, D_MODEL, 2 * D_FF), D_MODEL ** -0.5),
        "ffn_conv_w": nrm(ks[22], (DEPTH, FFN_CONV, 2 * D_FF), FFN_CONV ** -0.5),
        "ffn_conv_b": nrm(ks[23], (DEPTH, 2 * D_FF), 0.02),
        "w_down": nrm(ks[24], (DEPTH, D_FF, D_MODEL), D_FF ** -0.5),
        "norm3_g": 1.0 + nrm(ks[25], (DEPTH, D_MODEL), 0.02),
        "w_pe": nrm(ks[26], (DEPTH, PLE_DIM, D_MODEL), PLE_DIM ** -0.5),
        "w_pg": nrm(ks[27], (DEPTH, D_MODEL, D_MODEL), D_MODEL ** -0.5),
        "final_g": 1.0 + nrm(ks[28], (D_MODEL,), 0.02),
    }


def reference(x, p, norm1_g, w_in, a_ln_g, a_ln_b, a_ws, a_bs, b_conv_w, b_conv_b, b_wa, b_ba, b_wx, b_bx, b_lam, c_lb, c_norm_g, d_w, d_scale, w_out, norm2_g, w_up, ffn_conv_w, ffn_conv_b, w_down, norm3_g, w_pe, w_pg, final_g):
    lbs = jnp.cumsum(jax.nn.softmax(c_lb.astype(jnp.float32), axis=0), axis=0)
    lbs = lbs - lbs[0:1]
    for l in range(DEPTH):
        h = rms_norm(x, norm1_g[l])
        z = h @ w_in[l]
        y_a = gmlp_mixer(z[..., :OFF_B], a_ln_g[l], a_ln_b[l], a_ws[l], a_bs[l])
        y_b = rglru_mixer(z[..., OFF_B:OFF_B + W_GRP], z[..., OFF_B + W_GRP:OFF_C],
                          b_conv_w[l], b_conv_b[l], b_wa[l], b_ba[l], b_wx[l], b_bx[l], b_lam[l])
        zc = z[..., OFF_C:OFF_D]
        y_c = hgrn2_mixer(zc[..., :W_GRP], zc[..., W_GRP:2 * W_GRP], zc[..., 2 * W_GRP:3 * W_GRP],
                          zc[..., 3 * W_GRP:], lbs[l], c_norm_g[l])
        y_d = pool_mixer(z[..., OFF_D:], d_w[l], d_scale[l])
        mix = jnp.concatenate([y_a, y_b, y_c, y_d], axis=-1)
        x = x + mix @ w_out[l]
        hf = rms_norm(x, norm2_g[l]) @ w_up[l]
        hf = causal_dwconv(hf, ffn_conv_w[l], ffn_conv_b[l])
        gt, val = jnp.split(hf, 2, axis=-1)
        x = x + (jax.nn.gelu(gt) * val) @ w_down[l]
        gate = jax.nn.sigmoid(rms_norm(x, norm3_g[l]) @ w_pg[l])
        x = x + (p[l] @ w_pe[l]) * gate
    return rms_norm(x, final_g)
```

```python
import functools

import jax
import jax.numpy as jnp
from jax import lax
from jax.experimental import pallas as pl
from jax.experimental.pallas import tpu as pltpu

F32 = jnp.float32
BF16 = jnp.bfloat16

EPS = 1e-6
N_HEADS = 4
HEAD_DIM = 64
W_GRP = N_HEADS * HEAD_DIM
GMLP_CHUNK = 128
RGLRU_C = 8.0
POOL_WINDOWS = (2, 4, 8, 16)
POOL_HIST = 16
CONV_HIST = 8
LANES = 128
SUBLANES = 8
TS = 256
FF_CHUNK = 256
VMEM_LIMIT = 56 * 1024 * 1024


def _dot(a, b):
    return jnp.dot(a, b, preferred_element_type=F32)


def _dot_nt(a, b):
    return lax.dot_general(a, b, (((1,), (1,)), ((), ())), preferred_element_type=F32)


def _dot_tn(a, b):
    return lax.dot_general(a, b, (((0,), (0,)), ((), ())), preferred_element_type=F32)


def _rms_norm(x, g):
    return x * lax.rsqrt(jnp.mean(x * x, axis=-1, keepdims=True) + EPS) * g


def _iota(shape, axis):
    return lax.broadcasted_iota(jnp.int32, shape, axis)


def _split3(x):
    p1 = x.astype(BF16)
    r1 = x - p1.astype(F32)
    p2 = r1.astype(BF16)
    p3 = (r1 - p2.astype(F32)).astype(BF16)
    return p1, p2, p3


def _gmlp(z_a, ln_g, ln_b, ws_cat, bs_x):
    ts = z_a.shape[0]
    ab = jax.nn.gelu(z_a)
    u, v = ab[:, :W_GRP], ab[:, W_GRP:]
    mu = jnp.mean(v, axis=-1, keepdims=True)
    var = jnp.mean(jnp.square(v - mu), axis=-1, keepdims=True)
    vn = (v - mu) * lax.rsqrt(var + EPS) * ln_g + ln_b
    t_idx = _iota(ws_cat.shape, 0)
    s_idx = _iota(ws_cat.shape, 1) % GMLP_CHUNK
    wm = jnp.where(s_idx <= t_idx, ws_cat, 0.0).astype(BF16)
    head = _iota((GMLP_CHUNK, W_GRP), 1) // HEAD_DIM
    outs = []
    for c in range(ts // GMLP_CHUNK):
        vc = vn[c * GMLP_CHUNK:(c + 1) * GMLP_CHUNK]
        rhs = jnp.concatenate([jnp.where(head == h, vc, 0.0) for h in range(N_HEADS)], axis=0).astype(BF16)
        sv = _dot(wm, rhs) + bs_x
        outs.append(u[c * GMLP_CHUNK:(c + 1) * GMLP_CHUNK] * sv)
    return jnp.concatenate(outs, axis=0)


def _rglru(z_b, conv_w, conv_b, wax, bax, lam, conv_hist, h_carry):
    ts = z_b.shape[0]
    xb, gb = z_b[:, :W_GRP], z_b[:, W_GRP:]
    xe = jnp.concatenate([conv_hist[...], xb], axis=0)
    conv_hist[...] = xb[ts - CONV_HIST:]
    k_width = conv_w.shape[0]
    xc = conv_b + conv_w[k_width - 1:k_width] * xb
    for k in range(k_width - 1):
        shift = k_width - 1 - k
        xc = xc + conv_w[k:k + 1] * pltpu.roll(xe, shift, 0)[CONV_HIST:]
    ri = jax.nn.sigmoid(_dot(xc.astype(BF16), wax) + bax)
    r, i = ri[:, :W_GRP], ri[:, W_GRP:]
    log_a = (-RGLRU_C) * r * jax.nn.softplus(-lam)
    a = jnp.exp(log_a)
    th = jnp.tanh(log_a)
    mult = jnp.sqrt(-2.0 * th / (1.0 - th))
    bt = mult * (i * xc)
    row = _iota(a.shape, 0)
    shift = 1
    while shift < ts:
        keep = row >= shift
        a_sh = jnp.where(keep, pltpu.roll(a, shift, 0), 1.0)
        b_sh = jnp.where(keep, pltpu.roll(bt, shift, 0), 0.0)
        bt = a * b_sh + bt
        a = a * a_sh
        shift *= 2
    h = a * h_carry[0:1, :] + bt
    h_carry[...] = jnp.broadcast_to(h[ts - 1:ts], h_carry.shape)
    return h * jax.nn.gelu(gb)


_LEVELS = ((256, 64), (64, 16), (16, 4), (4, 1))


def _ref_rows(bd, outer, inner, j):
    ts, lanes = bd.shape
    row = j * inner
    if outer >= SUBLANES:
        b3 = bd.reshape(ts // outer, outer, lanes)
        return jnp.broadcast_to(b3[:, row:row + 1, :], b3.shape).reshape(ts, lanes)
    b3 = bd.reshape(ts // SUBLANES, SUBLANES, lanes)
    sub = _iota(b3.shape, 1)
    lo = jnp.broadcast_to(b3[:, row:row + 1, :], b3.shape)
    hi = jnp.broadcast_to(b3[:, outer + row:outer + row + 1, :], b3.shape)
    return jnp.where(sub < outer, lo, hi).reshape(ts, lanes)


def _hgrn2_head_att(qd, kd, bd):
    ts = qd.shape[0]
    t_idx = _iota((ts, LANES), 0)
    lo_half = _iota((ts, LANES), 1) < HEAD_DIM
    tt = _iota((ts, ts), 0)
    ss = _iota((ts, ts), 1)
    att = None
    for outer, inner in _LEVELS:
        slots = (0, 1, 2, 3) if inner == 1 else (1, 2, 3, None)
        pos = (t_idx % outer) // inner
        qcols, kcols = [], []
        for ja, jb in (slots[0:2], slots[2:4]):
            ra = _ref_rows(bd, outer, inner, ja)
            if jb is None:
                ref, jsel = ra, jnp.where(lo_half, ja, -1)
            else:
                ref = jnp.where(lo_half, ra, _ref_rows(bd, outer, inner, jb))
                jsel = jnp.where(lo_half, ja, jb)
            qmask = pos == jsel
            kmask = (pos <= jsel) if inner == 1 else (pos < jsel)
            qcols.append(jnp.where(qmask, qd * jnp.exp(bd - ref), 0.0))
            kcols.append(jnp.where(kmask, kd * jnp.exp(ref - bd), 0.0))
        q_l = jnp.concatenate(qcols, axis=1).astype(BF16)
        k_l = jnp.concatenate(kcols, axis=1).astype(BF16)
        a_l = _dot_nt(q_l, k_l)
        if outer < ts:
            a_l = jnp.where((tt // outer) == (ss // outer), a_l, 0.0)
        att = a_l if att is None else att + a_l
    return att


def _dup_heads(x):
    out = []
    lo_half = _iota((x.shape[0], LANES), 1) < HEAD_DIM
    for pair in range(N_HEADS // 2):
        xp = x[:, pair * LANES:(pair + 1) * LANES]
        rot = pltpu.roll(xp, HEAD_DIM, 1)
        out.append(jnp.where(lo_half, xp, rot))
        out.append(jnp.where(lo_half, rot, xp))
    return out


def _hgrn2(z_c, lb, norm_g, state_t):
    ts = z_c.shape[0]
    q, f, v, g = (z_c[:, n * W_GRP:(n + 1) * W_GRP] for n in range(4))
    qf = q * jax.nn.sigmoid(q)
    fgate = lb + (1.0 - lb) * jax.nn.sigmoid(f)
    logf = jnp.log(fgate)
    kf = 1.0 - fgate
    tri = (_iota((ts, ts), 1) <= _iota((ts, ts), 0)).astype(BF16)
    p1, p2, p3 = _split3(logf)
    bb = _dot(tri, p1) + _dot(tri, p2) + _dot(tri, p3)

    head = _iota((ts, W_GRP), 1) // HEAD_DIM
    v16 = v.astype(BF16)
    qh, kh, bh = _dup_heads(qf), _dup_heads(kf), _dup_heads(bb)
    o = _dot_nt((qf * jnp.exp(bb)).astype(BF16), state_t[...].astype(BF16))
    for h in range(N_HEADS):
        att = _hgrn2_head_att(qh[h], kh[h], bh[h])
        o = o + _dot(att.astype(BF16), jnp.where(head == h, v16, jnp.zeros_like(v16)))

    bl = bb[ts - 1:ts, :]
    k_hat = (kf * jnp.exp(bl - bb)).astype(BF16)
    upd = _dot_tn(v16, k_hat)
    same_head = (_iota(upd.shape, 0) // HEAD_DIM) == (_iota(upd.shape, 1) // HEAD_DIM)
    state_t[...] = state_t[...] * jnp.exp(bl) + jnp.where(same_head, upd, 0.0)

    avg = jnp.where(same_head, 1.0 / HEAD_DIM, 0.0).astype(BF16)
    oo = o * o
    o_hi = oo.astype(BF16)
    o_lo = (oo - o_hi.astype(F32)).astype(BF16)
    ms = _dot(o_hi, avg) + _dot(o_lo, avg)
    o = o * lax.rsqrt(ms + EPS) * norm_g
    return o * (g * jax.nn.sigmoid(g))


def _pool(xd, dw, scale, pool_hist, tile_start):
    ts = xd.shape[0]
    xe = jnp.concatenate([pool_hist[...], xd], axis=0)
    pool_hist[...] = xd[ts - POOL_HIST:]
    sums = []
    acc, span = xe, 1
    for w in POOL_WINDOWS:
        while span < w:
            acc = acc + pltpu.roll(acc, span, 0)
            span *= 2
        sums.append(acc[POOL_HIST:])
    grp = _iota(xd.shape, 1) // HEAD_DIM
    wsum, wlen = sums[-1], jnp.full(xd.shape, float(POOL_WINDOWS[-1]), F32)
    for j in range(len(POOL_WINDOWS) - 2, -1, -1):
        wsum = jnp.where(grp == j, sums[j], wsum)
        wlen = jnp.where(grp == j, float(POOL_WINDOWS[j]), wlen)
    pos = (tile_start + _iota(xd.shape, 0) + 1).astype(F32)
    pooled = wsum / jnp.minimum(pos, wlen) - xd
    return _dot(pooled.astype(BF16), dw) * scale


def _mixer_kernel(layer, x_ref, g1_ref, win_ref, alng_ref, alnb_ref, aws_ref, absx_ref,
                  bcw_ref, bcb_ref, bwax_ref, bbax_ref, blam_ref, clb_ref, cng_ref,
                  dw_ref, dsc_ref, wout_ref, o_ref,
                  conv_hist, h_carry, state_t, pool_hist):
    s = pl.program_id(1)

    @pl.when(s == 0)
    def _():
        conv_hist[...] = jnp.zeros_like(conv_hist)
        h_carry[...] = jnp.zeros_like(h_carry)
        state_t[...] = jnp.zeros_like(state_t)
        pool_hist[...] = jnp.zeros_like(pool_hist)

    x = x_ref[...]
    ts = x.shape[0]
    h = _rms_norm(x, g1_ref[...]).astype(BF16)
    off_b, off_c, off_d = 2 * W_GRP, 4 * W_GRP, 8 * W_GRP

    y_a = _gmlp(_dot(h, win_ref[:, 0:off_b]), alng_ref[...], alnb_ref[...], aws_ref[...], absx_ref[...])
    y_b = _rglru(_dot(h, win_ref[:, off_b:off_c]), bcw_ref[...], bcb_ref[...], bwax_ref[...], bbax_ref[...],
                 blam_ref[...], conv_hist, h_carry)

    clb = clb_ref[...]
    e = jnp.exp(clb - jnp.max(clb, axis=0, keepdims=True))
    sm = e / jnp.sum(e, axis=0, keepdims=True)
    lb = jnp.zeros((1, W_GRP), F32)
    for j in range(1, layer + 1):
        lb = lb + sm[j:j + 1, :]
    y_c = _hgrn2(_dot(h, win_ref[:, off_c:off_d]), lb, cng_ref[...], state_t)

    y_d = _pool(_dot(h, win_ref[:, off_d:off_d + W_GRP]), dw_ref[...], dsc_ref[...], pool_hist, s * ts)

    mix = jnp.concatenate([y_a, y_b, y_c, y_d], axis=1).astype(BF16)
    o_ref[...] = x + _dot(mix, wout_ref[...])


def _ffn_kernel(final, x_ref, p_ref, g2_ref, wup_ref, cw_ref, cb_ref, wdown_ref, g3_ref, wpe_ref, wpg_ref,
                gf_ref, o_ref, hist):
    s = pl.program_id(1)

    @pl.when(s == 0)
    def _():
        hist[...] = jnp.zeros_like(hist)

    x = x_ref[...]
    ts = x.shape[0]
    d_ff = wdown_ref.shape[0]
    k_width = cw_ref.shape[0]
    h2 = _rms_norm(x, g2_ref[...]).astype(BF16)

    def conv(col0):
        cols = slice(col0, col0 + FF_CHUNK)
        hf = _dot(h2, wup_ref[:, cols])
        xe = jnp.concatenate([hist[:, cols], hf], axis=0)
        hist[:, cols] = hf[ts - CONV_HIST:]
        y = cb_ref[:, cols] + cw_ref[k_width - 1:k_width, cols] * hf
        for k in range(k_width - 1):
            y = y + cw_ref[k:k + 1, cols] * pltpu.roll(xe, k_width - 1 - k, 0)[CONV_HIST:]
        return y

    acc = x
    for j in range(d_ff // FF_CHUNK):
        gt = conv(j * FF_CHUNK)
        val = conv(d_ff + j * FF_CHUNK)
        act = (jax.nn.gelu(gt) * val).astype(BF16)
        acc = acc + _dot(act, wdown_ref[j * FF_CHUNK:(j + 1) * FF_CHUNK, :])
    x = acc

    gate = jax.nn.sigmoid(_dot(_rms_norm(x, g3_ref[...]).astype(BF16), wpg_ref[...]))
    x = x + _dot(p_ref[...].astype(BF16), wpe_ref[...]) * gate
    if final:
        x = _rms_norm(x, gf_ref[...])
    o_ref[...] = x


def _const_spec(shape):
    return pl.BlockSpec(shape, lambda b, s: (0,) * len(shape), pipeline_mode=pl.Buffered(1))


def _tile_spec(width):
    return pl.BlockSpec((None, TS, width), lambda b, s: (b, s, 0))


def _params():
    return pltpu.CompilerParams(dimension_semantics=("arbitrary", "arbitrary"), vmem_limit_bytes=VMEM_LIMIT)


def _mixer_call(layer, x, consts):
    bsz, seq, d = x.shape
    return pl.pallas_call(
        functools.partial(_mixer_kernel, layer),
        out_shape=jax.ShapeDtypeStruct(x.shape, x.dtype),
        grid=(bsz, seq // TS),
        in_specs=[_tile_spec(d)] + [_const_spec(c.shape) for c in consts],
        out_specs=_tile_spec(d),
        scratch_shapes=[
            pltpu.VMEM((CONV_HIST, W_GRP), F32),
            pltpu.VMEM((SUBLANES, W_GRP), F32),
            pltpu.VMEM((W_GRP, W_GRP), F32),
            pltpu.VMEM((POOL_HIST, W_GRP), F32),
        ],
        compiler_params=_params(),
        name=f"mixer_l{layer}",
    )(x, *consts)


def _ffn_call(layer, final, x, p_l, consts, d_ff):
    bsz, seq, d = x.shape
    return pl.pallas_call(
        functools.partial(_ffn_kernel, final),
        out_shape=jax.ShapeDtypeStruct(x.shape, x.dtype),
        grid=(bsz, seq // TS),
        in_specs=[_tile_spec(d), _tile_spec(p_l.shape[-1])] + [_const_spec(c.shape) for c in consts],
        out_specs=_tile_spec(d),
        scratch_shapes=[pltpu.VMEM((CONV_HIST, 2 * d_ff), F32)],
        compiler_params=_params(),
        name=f"ffn_l{layer}",
    )(x, p_l, *consts)


def _block_diag(w):
    return jax.scipy.linalg.block_diag(*[w[h] for h in range(w.shape[0])])


def _row(v):
    return v.reshape(1, -1).astype(F32)


def kernel(x, p, norm1_g, w_in, a_ln_g, a_ln_b, a_ws, a_bs, b_conv_w, b_conv_b, b_wa, b_ba, b_wx, b_bx, b_lam, c_lb, c_norm_g, d_w, d_scale, w_out, norm2_g, w_up, ffn_conv_w, ffn_conv_b, w_down, norm3_g, w_pe, w_pg, final_g):
    depth = w_in.shape[0]
    d_ff = w_down.shape[1]
    assert x.shape[1] % TS == 0 and d_ff % FF_CHUNK == 0
    for l in range(depth):
        mixer_consts = (
            _row(norm1_g[l]),
            w_in[l].astype(BF16),
            _row(a_ln_g[l]), _row(a_ln_b[l]),
            a_ws[l].transpose(1, 0, 2).reshape(GMLP_CHUNK, N_HEADS * GMLP_CHUNK).astype(F32),
            jnp.repeat(a_bs[l].T, HEAD_DIM, axis=1).astype(F32),
            b_conv_w[l].astype(F32), _row(b_conv_b[l]),
            jnp.concatenate([_block_diag(b_wa[l]), _block_diag(b_wx[l])], axis=1).astype(BF16),
            _row(jnp.concatenate([b_ba[l], b_bx[l]])),
            _row(b_lam[l]),
            c_lb.astype(F32),
            _row(jnp.tile(c_norm_g[l], N_HEADS)),
            _block_diag(d_w[l]).astype(BF16), _row(d_scale[l]),
            w_out[l].astype(BF16),
        )
        x = _mixer_call(l, x, mixer_consts)
        ffn_consts = (
            _row(norm2_g[l]),
            w_up[l].astype(BF16),
            ffn_conv_w[l].astype(F32), _row(ffn_conv_b[l]),
            w_down[l].astype(BF16),
            _row(norm3_g[l]),
            w_pe[l].astype(BF16), w_pg[l].astype(BF16),
            _row(final_g),
        )
        x = _ffn_call(l, l == depth - 1, x, p[l], ffn_consts, d_ff)
    return x
```

```python
import functools

import jax
import jax.numpy as jnp
from jax import lax
from jax.experimental import pallas as pl
from jax.experimental.pallas import tpu as pltpu

F32 = jnp.float32
BF16 = jnp.bfloat16

EPS = 1e-6
N_HEADS = 4
HEAD_DIM = 64
W_GRP = N_HEADS * HEAD_DIM
GMLP_CHUNK = 128
RGLRU_C = 8.0
POOL_WINDOWS = (2, 4, 8, 16)
POOL_HIST = 16
CONV_HIST = 8
LANES = 128
SUBLANES = 8
TS = 256
SCAN_SEG = TS // SUBLANES
SCAN_PITCH = SCAN_SEG + 1
FF_CHUNK = 256
LOG2E = 1.4426950408889634
VMEM_LIMIT = 56 * 1024 * 1024


def _dot(a, b):
    return jnp.dot(a, b, preferred_element_type=F32)


def _dot_nt(a, b):
    return lax.dot_general(a, b, (((1,), (1,)), ((), ())), preferred_element_type=F32)


def _dot_tn(a, b):
    return lax.dot_general(a, b, (((0,), (0,)), ((), ())), preferred_element_type=F32)


def _rms_norm(x, g):
    return x * lax.rsqrt(jnp.mean(x * x, axis=-1, keepdims=True) + EPS) * g


def _iota(shape, axis):
    return lax.broadcasted_iota(jnp.int32, shape, axis)


def _gelu_tanh(x):
    c = 0.7978845608028654
    inner = x * (c + (c * 0.044715) * (x * x))
    return x * (0.5 + 0.5 * jnp.tanh(inner))


def _split3(x):
    p1 = x.astype(BF16)
    r1 = x - p1.astype(F32)
    p2 = r1.astype(BF16)
    p3 = (r1 - p2.astype(F32)).astype(BF16)
    return p1, p2, p3


def _gmlp(z_a, ln_g, ln_b, ws_cat, bs_x):
    ts = z_a.shape[0]
    ab = _gelu_tanh(z_a)
    u, v = ab[:, :W_GRP], ab[:, W_GRP:]
    mu = jnp.mean(v, axis=-1, keepdims=True)
    var = jnp.mean(jnp.square(v - mu), axis=-1, keepdims=True)
    vn = (v - mu) * lax.rsqrt(var + EPS) * ln_g + ln_b
    t_idx = _iota(ws_cat.shape, 0)
    s_idx = _iota(ws_cat.shape, 1) % GMLP_CHUNK
    wm = jnp.where(s_idx <= t_idx, ws_cat, 0.0).astype(BF16)
    head = _iota((GMLP_CHUNK, W_GRP), 1) // HEAD_DIM
    outs = []
    for c in range(ts // GMLP_CHUNK):
        vc = vn[c * GMLP_CHUNK:(c + 1) * GMLP_CHUNK]
        rhs = jnp.concatenate([jnp.where(head == h, vc, 0.0) for h in range(N_HEADS)], axis=0).astype(BF16)
        sv = _dot(wm, rhs) + bs_x
        outs.append(u[c * GMLP_CHUNK:(c + 1) * GMLP_CHUNK] * sv)
    return jnp.concatenate(outs, axis=0)


def _causal_conv_slab(slab, x_tile, w, b):
    ts = x_tile.shape[0]
    k_width = w.shape[0]
    slab[CONV_HIST:CONV_HIST + ts, :] = x_tile
    y = b + w[k_width - 1:k_width] * x_tile
    for k in range(k_width - 1):
        shift = k_width - 1 - k
        y = y + w[k:k + 1] * slab[CONV_HIST - shift:CONV_HIST - shift + ts, :]
    slab[0:CONV_HIST, :] = slab[ts:ts + CONV_HIST, :]
    return y


def _rglru(z_b, conv_w, conv_b, wax, bax, lam, conv_slab, a_slab, b_slab, h_carry):
    ts = z_b.shape[0]
    n_tiles = W_GRP // LANES
    xb, gb = z_b[:, :W_GRP], z_b[:, W_GRP:]
    xc = jnp.concatenate(
        [_causal_conv_slab(conv_slab.at[c], xb[:, c * LANES:(c + 1) * LANES],
                           conv_w[:, c * LANES:(c + 1) * LANES], conv_b[:, c * LANES:(c + 1) * LANES])
         for c in range(n_tiles)], axis=1)
    ri = jax.nn.sigmoid(_dot(xc.astype(BF16), wax) + bax)
    r, i = ri[:, :W_GRP], ri[:, W_GRP:]
    log_a = (-RGLRU_C) * r * jax.nn.softplus(-lam)
    a = jnp.exp(log_a)
    th = jnp.tanh(log_a)
    mult = jnp.sqrt(-2.0 * th / (1.0 - th))
    bt = mult * (i * xc)
    assert ts == SUBLANES * SCAN_SEG
    hs = []
    for c in range(n_tiles):
        lanes = slice(c * LANES, (c + 1) * LANES)
        for sg in range(SUBLANES):
            rows = slice(sg * SCAN_SEG, (sg + 1) * SCAN_SEG)
            a_slab[c, sg * SCAN_PITCH:sg * SCAN_PITCH + SCAN_SEG, :] = a[rows, lanes]
            b_slab[c, sg * SCAN_PITCH:sg * SCAN_PITCH + SCAN_SEG, :] = bt[rows, lanes]
        a_cum = h_loc = None
        for j in range(SCAN_SEG):
            col = pl.ds(j, SUBLANES, stride=SCAN_PITCH)
            a_j, b_j = a_slab[c, col, :], b_slab[c, col, :]
            if j == 0:
                a_cum, h_loc = a_j, b_j
            else:
                h_loc = a_j * h_loc + b_j
                a_cum = a_j * a_cum
            a_slab[c, col, :] = a_cum
            b_slab[c, col, :] = h_loc
        carry = h_carry[0:1, lanes]
        segs = []
        for sg in range(SUBLANES):
            rows = slice(sg * SCAN_PITCH, sg * SCAN_PITCH + SCAN_SEG)
            segs.append(b_slab[c, rows, :] + a_slab[c, rows, :] * carry)
            carry = a_cum[sg:sg + 1] * carry + h_loc[sg:sg + 1]
        h_carry[:, lanes] = jnp.broadcast_to(carry, (h_carry.shape[0], LANES))
        hs.append(jnp.concatenate(segs, axis=0))
    h = jnp.concatenate(hs, axis=1)
    return h * _gelu_tanh(gb)


_LEVELS = ((256, 64), (64, 16), (16, 4), (4, 1))


def _ref_rows(bd, outer, inner, j):
    ts, lanes = bd.shape
    row = j * inner
    if outer >= SUBLANES:
        b3 = bd.reshape(ts // outer, outer, lanes)
        return jnp.broadcast_to(b3[:, row:row + 1, :], b3.shape).reshape(ts, lanes)
    b3 = bd.reshape(ts // SUBLANES, SUBLANES, lanes)
    sub = _iota(b3.shape, 1)
    lo = jnp.broadcast_to(b3[:, row:row + 1, :], b3.shape)
    hi = jnp.broadcast_to(b3[:, outer + row:outer + row + 1, :], b3.shape)
    return jnp.where(sub < outer, lo, hi).reshape(ts, lanes)


def _level_slots(inner):
    slots = (0, 1, 2, 3) if inner == 1 else (1, 2, 3, None)
    return (slots[0:2], slots[2:4])


def _hgrn2_masks(ts):
    t_idx = _iota((ts, LANES), 0)
    lo_half = _iota((ts, LANES), 1) < HEAD_DIM
    tt = _iota((ts, ts), 0)
    ss = _iota((ts, ts), 1)
    col_masks, blk_masks = [], []
    for outer, inner in _LEVELS:
        pos = (t_idx % outer) // inner
        cols = []
        for ja, jb in _level_slots(inner):
            jsel = jnp.where(lo_half, ja, -1 if jb is None else jb)
            cols.append((pos == jsel, (pos <= jsel) if inner == 1 else (pos < jsel)))
        col_masks.append(cols)
        blk_masks.append(None if outer >= ts else (tt // outer) == (ss // outer))
    return lo_half, col_masks, blk_masks


def _hgrn2_head_att(qd, kd, bd, masks):
    lo_half, col_masks, blk_masks = masks
    att = None
    for (outer, inner), cols, blk in zip(_LEVELS, col_masks, blk_masks):
        qcols, kcols = [], []
        for (ja, jb), (qmask, kmask) in zip(_level_slots(inner), cols):
            ref = _ref_rows(bd, outer, inner, ja)
            if jb is not None:
                ref = jnp.where(lo_half, ref, _ref_rows(bd, outer, inner, jb))
            qcols.append(jnp.where(qmask, qd * jnp.exp2(bd - ref), 0.0))
            kcols.append(jnp.where(kmask, kd * jnp.exp2(ref - bd), 0.0))
        q_l = jnp.concatenate(qcols, axis=1).astype(BF16)
        k_l = jnp.concatenate(kcols, axis=1).astype(BF16)
        a_l = _dot_nt(q_l, k_l)
        if blk is not None:
            a_l = jnp.where(blk, a_l, 0.0)
        att = a_l if att is None else att + a_l
    return att


def _dup_heads(x):
    out = []
    lo_half = _iota((x.shape[0], LANES), 1) < HEAD_DIM
    for pair in range(N_HEADS // 2):
        xp = x[:, pair * LANES:(pair + 1) * LANES]
        rot = pltpu.roll(xp, HEAD_DIM, 1)
        out.append(jnp.where(lo_half, xp, rot))
        out.append(jnp.where(lo_half, rot, xp))
    return out


def _hgrn2(z_c, lb, norm_g, state_t):
    ts = z_c.shape[0]
    q, f, v, g = (z_c[:, n * W_GRP:(n + 1) * W_GRP] for n in range(4))
    qf = q * jax.nn.sigmoid(q)
    fgate = lb + (1.0 - lb) * jax.nn.sigmoid(f)
    log2f = jnp.log(fgate) * LOG2E
    kf = 1.0 - fgate
    tri = (_iota((ts, ts), 1) <= _iota((ts, ts), 0)).astype(BF16)
    p1, p2, p3 = _split3(log2f)
    bb = _dot(tri, p1) + _dot(tri, p2) + _dot(tri, p3)

    head = _iota((ts, W_GRP), 1) // HEAD_DIM
    v16 = v.astype(BF16)
    qh, kh, bh = _dup_heads(qf), _dup_heads(kf), _dup_heads(bb)
    masks = _hgrn2_masks(ts)
    o = _dot_nt((qf * jnp.exp2(bb)).astype(BF16), state_t[...].astype(BF16))
    for h in range(N_HEADS):
        att = _hgrn2_head_att(qh[h], kh[h], bh[h], masks)
        o = o + _dot(att.astype(BF16), jnp.where(head == h, v16, jnp.zeros_like(v16)))

    bl = bb[ts - 1:ts, :]
    k_hat = (kf * jnp.exp2(bl - bb)).astype(BF16)
    upd = _dot_tn(v16, k_hat)
    same_head = (_iota(upd.shape, 0) // HEAD_DIM) == (_iota(upd.shape, 1) // HEAD_DIM)
    state_t[...] = state_t[...] * jnp.exp2(bl) + jnp.where(same_head, upd, 0.0)

    avg = jnp.where(same_head, 1.0 / HEAD_DIM, 0.0).astype(BF16)
    oo = o * o
    o_hi = oo.astype(BF16)
    o_lo = (oo - o_hi.astype(F32)).astype(BF16)
    ms = _dot(o_hi, avg) + _dot(o_lo, avg)
    o = o * lax.rsqrt(ms + EPS) * norm_g
    return o * (g * jax.nn.sigmoid(g))


def _pool(xd, dw, scale, pool_slab, tile_start):
    ts = xd.shape[0]
    lo_half = _iota((ts, LANES), 1) < HEAD_DIM
    pos = (tile_start + _iota((ts, LANES), 0) + 1).astype(F32)
    pooled = []
    for c in range(W_GRP // LANES):
        slab = pool_slab.at[c]
        x_tile = xd[:, c * LANES:(c + 1) * LANES]
        slab[POOL_HIST:POOL_HIST + ts, :] = x_tile
        w_lo, w_hi = POOL_WINDOWS[2 * c], POOL_WINDOWS[2 * c + 1]
        if w_lo % SUBLANES == 0 and w_hi == 2 * w_lo:
            acc = slab[POOL_HIST - w_lo:POOL_HIST + ts, :]
            for k in range(1, w_lo):
                acc = acc + slab[POOL_HIST - w_lo - k:POOL_HIST + ts - k, :]
            s_lo = acc[w_lo:]
            s_hi = s_lo + acc[:ts]
        else:
            acc = x_tile
            for k in range(1, w_lo):
                acc = acc + slab[POOL_HIST - k:POOL_HIST - k + ts, :]
            s_lo = acc
            for k in range(w_lo, w_hi):
                acc = acc + slab[POOL_HIST - k:POOL_HIST - k + ts, :]
            s_hi = acc
        slab[0:POOL_HIST, :] = slab[ts:ts + POOL_HIST, :]
        wsum = jnp.where(lo_half, s_lo, s_hi)
        wlen = jnp.where(lo_half, float(w_lo), float(w_hi))
        pooled.append(wsum / jnp.minimum(pos, wlen) - x_tile)
    return _dot(jnp.concatenate(pooled, axis=1).astype(BF16), dw) * scale


def _mixer_kernel(layer, x_ref, g1_ref, win_ref, alng_ref, alnb_ref, aws_ref, absx_ref,
                  bcw_ref, bcb_ref, bwax_ref, bbax_ref, blam_ref, clb_ref, cng_ref,
                  dw_ref, dsc_ref, wout_ref, o_ref,
                  conv_slab, a_slab, b_slab, h_carry, state_t, pool_slab):
    s = pl.program_id(1)

    @pl.when(s == 0)
    def _():
        conv_slab[:, 0:CONV_HIST, :] = jnp.zeros((conv_slab.shape[0], CONV_HIST, LANES), F32)
        pool_slab[:, 0:POOL_HIST, :] = jnp.zeros((pool_slab.shape[0], POOL_HIST, LANES), F32)
        h_carry[...] = jnp.zeros_like(h_carry)
        state_t[...] = jnp.zeros_like(state_t)

    x = x_ref[...]
    ts = x.shape[0]
    h = _rms_norm(x, g1_ref[...]).astype(BF16)
    off_b, off_c, off_d = 2 * W_GRP, 4 * W_GRP, 8 * W_GRP

    y_a = _gmlp(_dot(h, win_ref[:, 0:off_b]), alng_ref[...], alnb_ref[...], aws_ref[...], absx_ref[...])
    y_b = _rglru(_dot(h, win_ref[:, off_b:off_c]), bcw_ref[...], bcb_ref[...], bwax_ref[...], bbax_ref[...],
                 blam_ref[...], conv_slab, a_slab, b_slab, h_carry)

    clb = clb_ref[...]
    e = jnp.exp(clb - jnp.max(clb, axis=0, keepdims=True))
    sm = e / jnp.sum(e, axis=0, keepdims=True)
    lb = jnp.zeros((1, W_GRP), F32)
    for j in range(1, layer + 1):
        lb = lb + sm[j:j + 1, :]
    y_c = _hgrn2(_dot(h, win_ref[:, off_c:off_d]), lb, cng_ref[...], state_t)

    y_d = _pool(_dot(h, win_ref[:, off_d:off_d + W_GRP]), dw_ref[...], dsc_ref[...], pool_slab, s * ts)

    mix = jnp.concatenate([y_a, y_b, y_c, y_d], axis=1).astype(BF16)
    o_ref[...] = x + _dot(mix, wout_ref[...])


def _ffn_kernel(final, x_ref, p_ref, g2_ref, wup_ref, cw_ref, cb_ref, wdown_ref, g3_ref, wpe_ref, wpg_ref,
                gf_ref, o_ref, hbuf):
    s = pl.program_id(1)

    @pl.when(s == 0)
    def _():
        hbuf[:, 0:CONV_HIST, :] = jnp.zeros((hbuf.shape[0], CONV_HIST, LANES), F32)

    x = x_ref[...]
    ts = x.shape[0]
    d_ff = wdown_ref.shape[0]
    k_width = cw_ref.shape[0]
    n_chunks = d_ff // FF_CHUNK
    h2 = _rms_norm(x, g2_ref[...]).astype(BF16)

    def up(col0):
        return _dot(h2, wup_ref[:, col0:col0 + FF_CHUNK])

    def conv(col0, hf):
        outs = []
        for c in range(FF_CHUNK // LANES):
            lanes = slice(col0 + c * LANES, col0 + (c + 1) * LANES)
            outs.append(_causal_conv_slab(hbuf.at[(col0 // LANES) + c], hf[:, c * LANES:(c + 1) * LANES],
                                          cw_ref[:, lanes], cb_ref[:, lanes]))
        return jnp.concatenate(outs, axis=1)

    acc = x
    hg_next, hv_next = up(0), up(d_ff)
    for j in range(n_chunks):
        hg, hv = hg_next, hv_next
        if j + 1 < n_chunks:
            hg_next, hv_next = up((j + 1) * FF_CHUNK), up(d_ff + (j + 1) * FF_CHUNK)
        gt = conv(j * FF_CHUNK, hg)
        val = conv(d_ff + j * FF_CHUNK, hv)
        act = (_gelu_tanh(gt) * val).astype(BF16)
        acc = acc + _dot(act, wdown_ref[j * FF_CHUNK:(j + 1) * FF_CHUNK, :])
    x = acc

    gate = jax.nn.sigmoid(_dot(_rms_norm(x, g3_ref[...]).astype(BF16), wpg_ref[...]))
    x = x + _dot(p_ref[...].astype(BF16), wpe_ref[...]) * gate
    if final:
        x = _rms_norm(x, gf_ref[...])
    o_ref[...] = x


def _const_spec(shape):
    return pl.BlockSpec(shape, lambda b, s: (0,) * len(shape), pipeline_mode=pl.Buffered(1))


def _tile_spec(width):
    return pl.BlockSpec((None, TS, width), lambda b, s: (b, s, 0))


def _params():
    return pltpu.CompilerParams(dimension_semantics=("arbitrary", "arbitrary"), vmem_limit_bytes=VMEM_LIMIT)


def _mixer_call(layer, x, consts):
    bsz, seq, d = x.shape
    return pl.pallas_call(
        functools.partial(_mixer_kernel, layer),
        out_shape=jax.ShapeDtypeStruct(x.shape, x.dtype),
        grid=(bsz, seq // TS),
        in_specs=[_tile_spec(d)] + [_const_spec(c.shape) for c in consts],
        out_specs=_tile_spec(d),
        scratch_shapes=[
            pltpu.VMEM((W_GRP // LANES, CONV_HIST + TS, LANES), F32),
            pltpu.VMEM((W_GRP // LANES, SUBLANES * SCAN_PITCH, LANES), F32),
            pltpu.VMEM((W_GRP // LANES, SUBLANES * SCAN_PITCH, LANES), F32),
            pltpu.VMEM((SUBLANES, W_GRP), F32),
            pltpu.VMEM((W_GRP, W_GRP), F32),
            pltpu.VMEM((W_GRP // LANES, POOL_HIST + TS, LANES), F32),
        ],
        compiler_params=_params(),
        name=f"mixer_l{layer}",
    )(x, *consts)


def _ffn_call(layer, final, x, p_l, consts, d_ff):
    bsz, seq, d = x.shape
    return pl.pallas_call(
        functools.partial(_ffn_kernel, final),
        out_shape=jax.ShapeDtypeStruct(x.shape, x.dtype),
        grid=(bsz, seq // TS),
        in_specs=[_tile_spec(d), _tile_spec(p_l.shape[-1])] + [_const_spec(c.shape) for c in consts],
        out_specs=_tile_spec(d),
        scratch_shapes=[pltpu.VMEM((2 * d_ff // LANES, CONV_HIST + TS, LANES), F32)],
        compiler_params=_params(),
        name=f"ffn_l{layer}",
    )(x, p_l, *consts)


def _block_diag(w):
    return jax.scipy.linalg.block_diag(*[w[h] for h in range(w.shape[0])])


def _row(v):
    return v.reshape(1, -1).astype(F32)


def kernel(x, p, norm1_g, w_in, a_ln_g, a_ln_b, a_ws, a_bs, b_conv_w, b_conv_b, b_wa, b_ba, b_wx, b_bx, b_lam, c_lb, c_norm_g, d_w, d_scale, w_out, norm2_g, w_up, ffn_conv_w, ffn_conv_b, w_down, norm3_g, w_pe, w_pg, final_g):
    depth = w_in.shape[0]
    d_ff = w_down.shape[1]
    assert x.shape[1] % TS == 0 and d_ff % FF_CHUNK == 0
    for l in range(depth):
        mixer_consts = (
            _row(norm1_g[l]),
            w_in[l].astype(BF16),
            _row(a_ln_g[l]), _row(a_ln_b[l]),
            a_ws[l].transpose(1, 0, 2).reshape(GMLP_CHUNK, N_HEADS * GMLP_CHUNK).astype(F32),
            jnp.repeat(a_bs[l].T, HEAD_DIM, axis=1).astype(F32),
            b_conv_w[l].astype(F32), _row(b_conv_b[l]),
            jnp.concatenate([_block_diag(b_wa[l]), _block_diag(b_wx[l])], axis=1).astype(BF16),
            _row(jnp.concatenate([b_ba[l], b_bx[l]])),
            _row(b_lam[l]),
            c_lb.astype(F32),
            _row(jnp.tile(c_norm_g[l], N_HEADS)),
            _block_diag(d_w[l]).astype(BF16), _row(d_scale[l]),
            w_out[l].astype(BF16),
        )
        x = _mixer_call(l, x, mixer_consts)
        ffn_consts = (
            _row(norm2_g[l]),
            w_up[l].astype(BF16),
            ffn_conv_w[l].astype(F32), _row(ffn_conv_b[l]),
            w_down[l].astype(BF16),
            _row(norm3_g[l]),
            w_pe[l].astype(BF16), w_pg[l].astype(BF16),
            _row(final_g),
        )
        x = _ffn_call(l, l == depth - 1, x, p[l], ffn_consts, d_ff)
    return x
```

```python
import functools

import jax
import jax.numpy as jnp
from jax import lax
from jax.experimental import pallas as pl
from jax.experimental.pallas import tpu as pltpu

F32 = jnp.float32
BF16 = jnp.bfloat16

EPS = 1e-6
N_HEADS = 4
HEAD_DIM = 64
W_GRP = N_HEADS * HEAD_DIM
GMLP_CHUNK = 128
RGLRU_C = 8.0
POOL_WINDOWS = (2, 4, 8, 16)
POOL_HIST = 16
CONV_HIST = 8
LANES = 128
SUBLANES = 8
TS = 256
SCAN_SEG = TS // SUBLANES
SCAN_PITCH = SCAN_SEG + 1
FF_CHUNK = 256
LOG2E = 1.4426950408889634
VMEM_LIMIT = 56 * 1024 * 1024


def _dot(a, b):
    return jnp.dot(a, b, preferred_element_type=F32)


def _dot_nt(a, b):
    return lax.dot_general(a, b, (((1,), (1,)), ((), ())), preferred_element_type=F32)


def _dot_tn(a, b):
    return lax.dot_general(a, b, (((0,), (0,)), ((), ())), preferred_element_type=F32)


def _rms_norm(x, g):
    return x * lax.rsqrt(jnp.mean(x * x, axis=-1, keepdims=True) + EPS) * g


def _iota(shape, axis):
    return lax.broadcasted_iota(jnp.int32, shape, axis)


def _gelu_tanh(x):
    c = 0.7978845608028654
    inner = x * (c + (c * 0.044715) * (x * x))
    return x * (0.5 + 0.5 * jnp.tanh(inner))


def _split3(x):
    p1 = x.astype(BF16)
    r1 = x - p1.astype(F32)
    p2 = r1.astype(BF16)
    p3 = (r1 - p2.astype(F32)).astype(BF16)
    return p1, p2, p3


def _gmlp(z_a, ln_g, ln_b, ws_cat, bs_x):
    ts = z_a.shape[0]
    ab = _gelu_tanh(z_a)
    u, v = ab[:, :W_GRP], ab[:, W_GRP:]
    mu = jnp.mean(v, axis=-1, keepdims=True)
    var = jnp.mean(jnp.square(v - mu), axis=-1, keepdims=True)
    vn = (v - mu) * lax.rsqrt(var + EPS) * ln_g + ln_b
    t_idx = _iota(ws_cat.shape, 0)
    s_idx = _iota(ws_cat.shape, 1) % GMLP_CHUNK
    wm = jnp.where(s_idx <= t_idx, ws_cat, 0.0).astype(BF16)
    head = _iota((GMLP_CHUNK, W_GRP), 1) // HEAD_DIM
    outs = []
    for c in range(ts // GMLP_CHUNK):
        vc = vn[c * GMLP_CHUNK:(c + 1) * GMLP_CHUNK]
        rhs = jnp.concatenate([jnp.where(head == h, vc, 0.0) for h in range(N_HEADS)], axis=0).astype(BF16)
        sv = _dot(wm, rhs) + bs_x
        outs.append(u[c * GMLP_CHUNK:(c + 1) * GMLP_CHUNK] * sv)
    return jnp.concatenate(outs, axis=0)


def _causal_conv_slab(slab, x_tile, w, b):
    ts = x_tile.shape[0]
    k_width = w.shape[0]
    slab[CONV_HIST:CONV_HIST + ts, :] = x_tile
    y = b + w[k_width - 1:k_width] * x_tile
    for k in range(k_width - 1):
        shift = k_width - 1 - k
        y = y + w[k:k + 1] * slab[CONV_HIST - shift:CONV_HIST - shift + ts, :]
    slab[0:CONV_HIST, :] = slab[ts:ts + CONV_HIST, :]
    return y


def _rglru(z_b, conv_w, conv_b, wax, bax, lam, conv_slab, a_slab, b_slab, h_carry):
    ts = z_b.shape[0]
    n_tiles = W_GRP // LANES
    xb, gb = z_b[:, :W_GRP], z_b[:, W_GRP:]
    xc = jnp.concatenate(
        [_causal_conv_slab(conv_slab.at[c], xb[:, c * LANES:(c + 1) * LANES],
                           conv_w[:, c * LANES:(c + 1) * LANES], conv_b[:, c * LANES:(c + 1) * LANES])
         for c in range(n_tiles)], axis=1)
    ri = jax.nn.sigmoid(_dot(xc.astype(BF16), wax) + bax)
    r, i = ri[:, :W_GRP], ri[:, W_GRP:]
    log_a = (-RGLRU_C) * r * jax.nn.softplus(-lam)
    a = jnp.exp(log_a)
    th = jnp.tanh(log_a)
    mult = jnp.sqrt(-2.0 * th / (1.0 - th))
    bt = mult * (i * xc)
    assert ts == SUBLANES * SCAN_SEG
    hs = []
    for c in range(n_tiles):
        lanes = slice(c * LANES, (c + 1) * LANES)
        for sg in range(SUBLANES):
            rows = slice(sg * SCAN_SEG, (sg + 1) * SCAN_SEG)
            a_slab[c, sg * SCAN_PITCH:sg * SCAN_PITCH + SCAN_SEG, :] = a[rows, lanes]
            b_slab[c, sg * SCAN_PITCH:sg * SCAN_PITCH + SCAN_SEG, :] = bt[rows, lanes]
        a_cum = h_loc = None
        for j in range(SCAN_SEG):
            col = pl.ds(j, SUBLANES, stride=SCAN_PITCH)
            a_j, b_j = a_slab[c, col, :], b_slab[c, col, :]
            if j == 0:
                a_cum, h_loc = a_j, b_j
            else:
                h_loc = a_j * h_loc + b_j
                a_cum = a_j * a_cum
            a_slab[c, col, :] = a_cum
            b_slab[c, col, :] = h_loc
        carry = h_carry[0:1, lanes]
        segs = []
        for sg in range(SUBLANES):
            rows = slice(sg * SCAN_PITCH, sg * SCAN_PITCH + SCAN_SEG)
            segs.append(b_slab[c, rows, :] + a_slab[c, rows, :] * carry)
            carry = a_cum[sg:sg + 1] * carry + h_loc[sg:sg + 1]
        h_carry[:, lanes] = jnp.broadcast_to(carry, (h_carry.shape[0], LANES))
        hs.append(jnp.concatenate(segs, axis=0))
    h = jnp.concatenate(hs, axis=1)
    return h * _gelu_tanh(gb)


_LEVELS = ((256, 64), (64, 16), (16, 4), (4, 1))


def _ref_rows(bd, outer, inner, j):
    ts, lanes = bd.shape
    row = j * inner
    if outer >= SUBLANES:
        b3 = bd.reshape(ts // outer, outer, lanes)
        return jnp.broadcast_to(b3[:, row:row + 1, :], b3.shape).reshape(ts, lanes)
    b3 = bd.reshape(ts // SUBLANES, SUBLANES, lanes)
    sub = _iota(b3.shape, 1)
    lo = jnp.broadcast_to(b3[:, row:row + 1, :], b3.shape)
    hi = jnp.broadcast_to(b3[:, outer + row:outer + row + 1, :], b3.shape)
    return jnp.where(sub < outer, lo, hi).reshape(ts, lanes)


def _level_slots(inner):
    slots = (0, 1, 2, 3) if inner == 1 else (1, 2, 3, None)
    return (slots[0:2], slots[2:4])


def _hgrn2_masks(ts):
    t_idx = _iota((ts, LANES), 0)
    lo_half = _iota((ts, LANES), 1) < HEAD_DIM
    tt = _iota((ts, ts), 0)
    ss = _iota((ts, ts), 1)
    col_masks, blk_masks = [], []
    for outer, inner in _LEVELS:
        pos = (t_idx % outer) // inner
        cols = []
        for ja, jb in _level_slots(inner):
            jsel = jnp.where(lo_half, ja, -1 if jb is None else jb)
            cols.append((pos == jsel, (pos <= jsel) if inner == 1 else (pos < jsel)))
        col_masks.append(cols)
        blk_masks.append(None if outer >= ts else (tt // outer) == (ss // outer))
    return lo_half, col_masks, blk_masks


def _hgrn2_head_att(qd, kd, bd, masks):
    lo_half, col_masks, blk_masks = masks
    att = None
    for (outer, inner), cols, blk in zip(_LEVELS, col_masks, blk_masks):
        qcols, kcols = [], []
        for (ja, jb), (qmask, kmask) in zip(_level_slots(inner), cols):
            ref = _ref_rows(bd, outer, inner, ja)
            if jb is not None:
                ref = jnp.where(lo_half, ref, _ref_rows(bd, outer, inner, jb))
            qcols.append(jnp.where(qmask, qd * jnp.exp2(bd - ref), 0.0))
            kcols.append(jnp.where(kmask, kd * jnp.exp2(ref - bd), 0.0))
        q_l = jnp.concatenate(qcols, axis=1).astype(BF16)
        k_l = jnp.concatenate(kcols, axis=1).astype(BF16)
        a_l = _dot_nt(q_l, k_l)
        if blk is not None:
            a_l = jnp.where(blk, a_l, 0.0)
        att = a_l if att is None else att + a_l
    return att


def _dup_heads(x):
    out = []
    lo_half = _iota((x.shape[0], LANES), 1) < HEAD_DIM
    for pair in range(N_HEADS // 2):
        xp = x[:, pair * LANES:(pair + 1) * LANES]
        rot = pltpu.roll(xp, HEAD_DIM, 1)
        out.append(jnp.where(lo_half, xp, rot))
        out.append(jnp.where(lo_half, rot, xp))
    return out


def _hgrn2(z_c, lb, norm_g, state_t):
    ts = z_c.shape[0]
    q, f, v, g = (z_c[:, n * W_GRP:(n + 1) * W_GRP] for n in range(4))
    qf = q * jax.nn.sigmoid(q)
    fgate = lb + (1.0 - lb) * jax.nn.sigmoid(f)
    log2f = jnp.log(fgate) * LOG2E
    kf = 1.0 - fgate
    tri = (_iota((ts, ts), 1) <= _iota((ts, ts), 0)).astype(BF16)
    p1, p2, p3 = _split3(log2f)
    bb = _dot(tri, p1) + _dot(tri, p2) + _dot(tri, p3)

    head = _iota((ts, W_GRP), 1) // HEAD_DIM
    v16 = v.astype(BF16)
    qh, kh, bh = _dup_heads(qf), _dup_heads(kf), _dup_heads(bb)
    masks = _hgrn2_masks(ts)
    yield
    o = _dot_nt((qf * jnp.exp2(bb)).astype(BF16), state_t[...].astype(BF16))
    for h in range(N_HEADS):
        att = _hgrn2_head_att(qh[h], kh[h], bh[h], masks)
        o = o + _dot(att.astype(BF16), jnp.where(head == h, v16, jnp.zeros_like(v16)))
        yield

    bl = bb[ts - 1:ts, :]
    k_hat = (kf * jnp.exp2(bl - bb)).astype(BF16)
    upd = _dot_tn(v16, k_hat)
    same_head = (_iota(upd.shape, 0) // HEAD_DIM) == (_iota(upd.shape, 1) // HEAD_DIM)
    state_t[...] = state_t[...] * jnp.exp2(bl) + jnp.where(same_head, upd, 0.0)

    avg = jnp.where(same_head, 1.0 / HEAD_DIM, 0.0).astype(BF16)
    oo = o * o
    o_hi = oo.astype(BF16)
    o_lo = (oo - o_hi.astype(F32)).astype(BF16)
    ms = _dot(o_hi, avg) + _dot(o_lo, avg)
    o = o * lax.rsqrt(ms + EPS) * norm_g
    return o * (g * jax.nn.sigmoid(g))


def _pool(xd, dw, scale, pool_slab, tile_start):
    ts = xd.shape[0]
    lo_half = _iota((ts, LANES), 1) < HEAD_DIM
    pos = (tile_start + _iota((ts, LANES), 0) + 1).astype(F32)
    pooled = []
    for c in range(W_GRP // LANES):
        slab = pool_slab.at[c]
        x_tile = xd[:, c * LANES:(c + 1) * LANES]
        slab[POOL_HIST:POOL_HIST + ts, :] = x_tile
        w_lo, w_hi = POOL_WINDOWS[2 * c], POOL_WINDOWS[2 * c + 1]
        if w_lo % SUBLANES == 0 and w_hi == 2 * w_lo:
            acc = slab[POOL_HIST - w_lo:POOL_HIST + ts, :]
            for k in range(1, w_lo):
                acc = acc + slab[POOL_HIST - w_lo - k:POOL_HIST + ts - k, :]
            s_lo = acc[w_lo:]
            s_hi = s_lo + acc[:ts]
        else:
            acc = x_tile
            for k in range(1, w_lo):
                acc = acc + slab[POOL_HIST - k:POOL_HIST - k + ts, :]
            s_lo = acc
            for k in range(w_lo, w_hi):
                acc = acc + slab[POOL_HIST - k:POOL_HIST - k + ts, :]
            s_hi = acc
        slab[0:POOL_HIST, :] = slab[ts:ts + POOL_HIST, :]
        wsum = jnp.where(lo_half, s_lo, s_hi)
        wlen = jnp.where(lo_half, float(w_lo), float(w_hi))
        pooled.append(wsum / jnp.minimum(pos, wlen) - x_tile)
    return _dot(jnp.concatenate(pooled, axis=1).astype(BF16), dw) * scale


def _mixer_steps(layer, tile_start, x, consts, scratch, out):
    (g1_ref, win_ref, alng_ref, alnb_ref, aws_ref, absx_ref, bcw_ref, bcb_ref, bwax_ref, bbax_ref, blam_ref,
     clb_ref, cng_ref, dw_ref, dsc_ref, wout_ref) = consts
    conv_slab, a_slab, b_slab, h_carry, state_t, pool_slab = scratch
    h = _rms_norm(x, g1_ref[...]).astype(BF16)
    off_b, off_c, off_d = 2 * W_GRP, 4 * W_GRP, 8 * W_GRP

    y_a = _gmlp(_dot(h, win_ref[:, 0:off_b]), alng_ref[...], alnb_ref[...], aws_ref[...], absx_ref[...])
    yield
    y_b = _rglru(_dot(h, win_ref[:, off_b:off_c]), bcw_ref[...], bcb_ref[...], bwax_ref[...], bbax_ref[...],
                 blam_ref[...], conv_slab, a_slab, b_slab, h_carry)
    yield
    clb = clb_ref[...]
    e = jnp.exp(clb - jnp.max(clb, axis=0, keepdims=True))
    sm = e / jnp.sum(e, axis=0, keepdims=True)
    lb = jnp.zeros((1, W_GRP), F32)
    for j in range(1, layer + 1):
        lb = lb + sm[j:j + 1, :]
    y_c = yield from _hgrn2(_dot(h, win_ref[:, off_c:off_d]), lb, cng_ref[...], state_t)
    yield
    y_d = _pool(_dot(h, win_ref[:, off_d:off_d + W_GRP]), dw_ref[...], dsc_ref[...], pool_slab, tile_start)
    yield
    mix = jnp.concatenate([y_a, y_b, y_c, y_d], axis=1).astype(BF16)
    out.append(x + _dot(mix, wout_ref[...]))


def _ffn_steps(final, x, p_tile, consts, hbuf, out):
    g2_ref, wup_ref, cw_ref, cb_ref, wdown_ref, g3_ref, wpe_ref, wpg_ref, gf_ref = consts
    d_ff = wdown_ref.shape[0]
    n_chunks = d_ff // FF_CHUNK
    h2 = _rms_norm(x, g2_ref[...]).astype(BF16)

    def up(col0):
        return _dot(h2, wup_ref[:, col0:col0 + FF_CHUNK])

    def conv(col0, hf):
        outs = []
        for c in range(FF_CHUNK // LANES):
            lanes = slice(col0 + c * LANES, col0 + (c + 1) * LANES)
            outs.append(_causal_conv_slab(hbuf.at[(col0 // LANES) + c], hf[:, c * LANES:(c + 1) * LANES],
                                          cw_ref[:, lanes], cb_ref[:, lanes]))
        return jnp.concatenate(outs, axis=1)

    acc = x
    hg_next, hv_next = up(0), up(d_ff)
    yield
    for j in range(n_chunks):
        hg, hv = hg_next, hv_next
        if j + 1 < n_chunks:
            hg_next, hv_next = up((j + 1) * FF_CHUNK), up(d_ff + (j + 1) * FF_CHUNK)
        gt = conv(j * FF_CHUNK, hg)
        val = conv(d_ff + j * FF_CHUNK, hv)
        act = (_gelu_tanh(gt) * val).astype(BF16)
        acc = acc + _dot(act, wdown_ref[j * FF_CHUNK:(j + 1) * FF_CHUNK, :])
        yield
    x = acc
    gate = jax.nn.sigmoid(_dot(_rms_norm(x, g3_ref[...]).astype(BF16), wpg_ref[...]))
    x = x + _dot(p_tile.astype(BF16), wpe_ref[...]) * gate
    if final:
        x = _rms_norm(x, gf_ref[...])
    out.append(x)


N_MIXER_CONSTS = 16
N_FFN_CONSTS = 9


def _layer_kernel(layer, final, x_ref, p_ref, *refs):
    mixer_consts = refs[:N_MIXER_CONSTS]
    ffn_consts = refs[N_MIXER_CONSTS:N_MIXER_CONSTS + N_FFN_CONSTS]
    o_ref = refs[N_MIXER_CONSTS + N_FFN_CONSTS]
    conv_slab, a_slab, b_slab, h_carry, state_t, pool_slab, hbuf, x1_buf = refs[N_MIXER_CONSTS + N_FFN_CONSTS + 1:]
    s = pl.program_id(1)

    @pl.when(s == 0)
    def _():
        conv_slab[:, 0:CONV_HIST, :] = jnp.zeros((conv_slab.shape[0], CONV_HIST, LANES), F32)
        pool_slab[:, 0:POOL_HIST, :] = jnp.zeros((pool_slab.shape[0], POOL_HIST, LANES), F32)
        h_carry[...] = jnp.zeros_like(h_carry)
        state_t[...] = jnp.zeros_like(state_t)
        x1_buf[...] = jnp.zeros_like(x1_buf)

    @pl.when(s <= 1)
    def _():
        hbuf[:, 0:CONV_HIST, :] = jnp.zeros((hbuf.shape[0], CONV_HIST, LANES), F32)

    ts = x_ref.shape[0]
    mixer_out, ffn_out = [], []
    gens = [
        _ffn_steps(final, x1_buf[(s + 1) % 2], p_ref[...], ffn_consts, hbuf, ffn_out),
        _mixer_steps(layer, s * ts, x_ref[...], mixer_consts,
                     (conv_slab, a_slab, b_slab, h_carry, state_t, pool_slab), mixer_out),
    ]
    while gens:
        for g in list(gens):
            try:
                next(g)
            except StopIteration:
                gens.remove(g)
    x1_buf[s % 2] = mixer_out[0]
    o_ref[...] = ffn_out[0]


def _layer_spec(arr, layer):
    zeros = (0,) * (arr.ndim - 1)
    return pl.BlockSpec((None,) + arr.shape[1:], lambda b, s: (layer,) + zeros, pipeline_mode=pl.Buffered(1))


def _params():
    return pltpu.CompilerParams(dimension_semantics=("arbitrary", "arbitrary"), vmem_limit_bytes=VMEM_LIMIT)


def _layer_call(layer, final, x, p, mixer_consts, ffn_consts):
    bsz, seq, d = x.shape
    n_tiles = seq // TS
    d_ff = ffn_consts[4].shape[1]
    consts = tuple(mixer_consts) + tuple(ffn_consts)
    assert len(mixer_consts) == N_MIXER_CONSTS and len(ffn_consts) == N_FFN_CONSTS
    return pl.pallas_call(
        functools.partial(_layer_kernel, layer, final),
        out_shape=jax.ShapeDtypeStruct(x.shape, x.dtype),
        grid=(bsz, n_tiles + 1),
        in_specs=[pl.BlockSpec((None, TS, d), lambda b, s: (b, jnp.minimum(s, n_tiles - 1), 0)),
                  pl.BlockSpec((None, None, TS, p.shape[-1]), lambda b, s: (layer, b, jnp.maximum(s - 1, 0), 0))]
        + [_layer_spec(c, layer) for c in consts],
        out_specs=pl.BlockSpec((None, TS, d), lambda b, s: (b, jnp.maximum(s - 1, 0), 0)),
        scratch_shapes=[
            pltpu.VMEM((W_GRP // LANES, CONV_HIST + TS, LANES), F32),
            pltpu.VMEM((W_GRP // LANES, SUBLANES * SCAN_PITCH, LANES), F32),
            pltpu.VMEM((W_GRP // LANES, SUBLANES * SCAN_PITCH, LANES), F32),
            pltpu.VMEM((SUBLANES, W_GRP), F32),
            pltpu.VMEM((W_GRP, W_GRP), F32),
            pltpu.VMEM((W_GRP // LANES, POOL_HIST + TS, LANES), F32),
            pltpu.VMEM((2 * d_ff // LANES, CONV_HIST + TS, LANES), F32),
            pltpu.VMEM((2, TS, d), F32),
        ],
        compiler_params=_params(),
        name=f"layer_l{layer}",
    )(x, p, *consts)


def _block_diag(w):
    depth, n_h, d_in, d_out = w.shape
    eye = jnp.eye(n_h, dtype=w.dtype)
    return jnp.einsum("lhde,hg->lhdge", w, eye).reshape(depth, n_h * d_in, n_h * d_out)


def _rows(v):
    return v.reshape(v.shape[0], 1, -1).astype(F32)


def kernel(x, p, norm1_g, w_in, a_ln_g, a_ln_b, a_ws, a_bs, b_conv_w, b_conv_b, b_wa, b_ba, b_wx, b_bx, b_lam, c_lb, c_norm_g, d_w, d_scale, w_out, norm2_g, w_up, ffn_conv_w, ffn_conv_b, w_down, norm3_g, w_pe, w_pg, final_g):
    depth = w_in.shape[0]
    d_ff = w_down.shape[1]
    assert x.shape[1] % TS == 0 and d_ff % FF_CHUNK == 0
    mixer_consts = (
        _rows(norm1_g),
        w_in.astype(BF16),
        _rows(a_ln_g), _rows(a_ln_b),
        a_ws.transpose(0, 2, 1, 3).reshape(depth, GMLP_CHUNK, N_HEADS * GMLP_CHUNK).astype(F32),
        jnp.repeat(a_bs.transpose(0, 2, 1), HEAD_DIM, axis=2).astype(F32),
        b_conv_w.astype(F32), _rows(b_conv_b),
        jnp.concatenate([_block_diag(b_wa), _block_diag(b_wx)], axis=2).astype(BF16),
        _rows(jnp.concatenate([b_ba, b_bx], axis=1)),
        _rows(b_lam),
        jnp.broadcast_to(c_lb.astype(F32), (depth,) + c_lb.shape),
        _rows(jnp.tile(c_norm_g, (1, N_HEADS))),
        _block_diag(d_w).astype(BF16), _rows(d_scale),
        w_out.astype(BF16),
    )
    ffn_consts = (
        _rows(norm2_g),
        w_up.astype(BF16),
        ffn_conv_w.astype(F32), _rows(ffn_conv_b),
        w_down.astype(BF16),
        _rows(norm3_g),
        w_pe.astype(BF16), w_pg.astype(BF16),
        jnp.broadcast_to(final_g.astype(F32).reshape(1, 1, -1), (depth, 1, final_g.shape[0])),
    )
    for l in range(depth):
        x = _layer_call(l, l == depth - 1, x, p, mixer_consts, ffn_consts)
    return x
```

```python
import functools

import jax
import jax.numpy as jnp
from jax import lax
from jax.experimental import pallas as pl
from jax.experimental.pallas import tpu as pltpu

F32 = jnp.float32
BF16 = jnp.bfloat16

EPS = 1e-6
N_HEADS = 4
HEAD_DIM = 64
W_GRP = N_HEADS * HEAD_DIM
GMLP_CHUNK = 128
RGLRU_C = 8.0
POOL_WINDOWS = (2, 4, 8, 16)
POOL_HIST = 16
CONV_HIST = 8
LANES = 128
SUBLANES = 8
TS = 256
SCAN_SEG = TS // SUBLANES
SCAN_PITCH = SCAN_SEG + 1
FF_CHUNK = 256
LOG2E = 1.4426950408889634
VMEM_LIMIT = 56 * 1024 * 1024


def _dot(a, b):
    return jnp.dot(a, b, preferred_element_type=F32)


def _dot_nt(a, b):
    return lax.dot_general(a, b, (((1,), (1,)), ((), ())), preferred_element_type=F32)


def _dot_tn(a, b):
    return lax.dot_general(a, b, (((0,), (0,)), ((), ())), preferred_element_type=F32)


def _rms_norm(x, g):
    return x * lax.rsqrt(jnp.mean(x * x, axis=-1, keepdims=True) + EPS) * g


def _iota(shape, axis):
    return lax.broadcasted_iota(jnp.int32, shape, axis)


def _gelu_tanh(x):
    c = 0.7978845608028654
    inner = x * (c + (c * 0.044715) * (x * x))
    return x * (0.5 + 0.5 * jnp.tanh(inner))


def _split3(x):
    p1 = x.astype(BF16)
    r1 = x - p1.astype(F32)
    p2 = r1.astype(BF16)
    p3 = (r1 - p2.astype(F32)).astype(BF16)
    return p1, p2, p3


def _gmlp(z_a, ln_g, ln_b, ws_cat, bs_x):
    ts = z_a.shape[0]
    ab = _gelu_tanh(z_a)
    u, v = ab[:, :W_GRP], ab[:, W_GRP:]
    mu = jnp.mean(v, axis=-1, keepdims=True)
    var = jnp.mean(jnp.square(v - mu), axis=-1, keepdims=True)
    vn = (v - mu) * lax.rsqrt(var + EPS) * ln_g + ln_b
    t_idx = _iota(ws_cat.shape, 0)
    s_idx = _iota(ws_cat.shape, 1) % GMLP_CHUNK
    wm = jnp.where(s_idx <= t_idx, ws_cat, 0.0).astype(BF16)
    head = _iota((GMLP_CHUNK, W_GRP), 1) // HEAD_DIM
    outs = []
    for c in range(ts // GMLP_CHUNK):
        vc = vn[c * GMLP_CHUNK:(c + 1) * GMLP_CHUNK]
        rhs = jnp.concatenate([jnp.where(head == h, vc, 0.0) for h in range(N_HEADS)], axis=0).astype(BF16)
        sv = _dot(wm, rhs) + bs_x
        outs.append(u[c * GMLP_CHUNK:(c + 1) * GMLP_CHUNK] * sv)
    return jnp.concatenate(outs, axis=0)


def _causal_conv_slab(slab, x_tile, w, b):
    ts = x_tile.shape[0]
    k_width = w.shape[0]
    slab[CONV_HIST:CONV_HIST + ts, :] = x_tile
    y = b + w[k_width - 1:k_width] * x_tile
    for k in range(k_width - 1):
        shift = k_width - 1 - k
        y = y + w[k:k + 1] * slab[CONV_HIST - shift:CONV_HIST - shift + ts, :]
    slab[0:CONV_HIST, :] = slab[ts:ts + CONV_HIST, :]
    return y


def _rglru(z_b, conv_w, conv_b, wax, bax, lam, conv_slab, a_slab, b_slab, h_carry):
    ts = z_b.shape[0]
    n_tiles = W_GRP // LANES
    xb, gb = z_b[:, :W_GRP], z_b[:, W_GRP:]
    xc = jnp.concatenate(
        [_causal_conv_slab(conv_slab.at[c], xb[:, c * LANES:(c + 1) * LANES],
                           conv_w[:, c * LANES:(c + 1) * LANES], conv_b[:, c * LANES:(c + 1) * LANES])
         for c in range(n_tiles)], axis=1)
    ri = jax.nn.sigmoid(_dot(xc.astype(BF16), wax) + bax)
    r, i = ri[:, :W_GRP], ri[:, W_GRP:]
    log_a = (-RGLRU_C) * r * jax.nn.softplus(-lam)
    a = jnp.exp(log_a)
    th = jnp.tanh(log_a)
    mult = jnp.sqrt(-2.0 * th / (1.0 - th))
    bt = mult * (i * xc)
    assert ts == SUBLANES * SCAN_SEG
    hs = []
    for c in range(n_tiles):
        lanes = slice(c * LANES, (c + 1) * LANES)
        for sg in range(SUBLANES):
            rows = slice(sg * SCAN_SEG, (sg + 1) * SCAN_SEG)
            a_slab[c, sg * SCAN_PITCH:sg * SCAN_PITCH + SCAN_SEG, :] = a[rows, lanes]
            b_slab[c, sg * SCAN_PITCH:sg * SCAN_PITCH + SCAN_SEG, :] = bt[rows, lanes]
        a_cum = h_loc = None
        for j in range(SCAN_SEG):
            col = pl.ds(j, SUBLANES, stride=SCAN_PITCH)
            a_j, b_j = a_slab[c, col, :], b_slab[c, col, :]
            if j == 0:
                a_cum, h_loc = a_j, b_j
            else:
                h_loc = a_j * h_loc + b_j
                a_cum = a_j * a_cum
            a_slab[c, col, :] = a_cum
            b_slab[c, col, :] = h_loc
        carry = h_carry[0:1, lanes]
        segs = []
        for sg in range(SUBLANES):
            rows = slice(sg * SCAN_PITCH, sg * SCAN_PITCH + SCAN_SEG)
            segs.append(b_slab[c, rows, :] + a_slab[c, rows, :] * carry)
            carry = a_cum[sg:sg + 1] * carry + h_loc[sg:sg + 1]
        h_carry[:, lanes] = jnp.broadcast_to(carry, (h_carry.shape[0], LANES))
        hs.append(jnp.concatenate(segs, axis=0))
    h = jnp.concatenate(hs, axis=1)
    return h * _gelu_tanh(gb)


_LEVELS = ((256, 64), (64, 16), (16, 4), (4, 1))


def _ref_rows(bd, outer, inner, j):
    ts, lanes = bd.shape
    row = j * inner
    if outer >= SUBLANES:
        b3 = bd.reshape(ts // outer, outer, lanes)
        return jnp.broadcast_to(b3[:, row:row + 1, :], b3.shape).reshape(ts, lanes)
    b3 = bd.reshape(ts // SUBLANES, SUBLANES, lanes)
    sub = _iota(b3.shape, 1)
    lo = jnp.broadcast_to(b3[:, row:row + 1, :], b3.shape)
    hi = jnp.broadcast_to(b3[:, outer + row:outer + row + 1, :], b3.shape)
    return jnp.where(sub < outer, lo, hi).reshape(ts, lanes)


def _level_slots(inner):
    slots = (0, 1, 2, 3) if inner == 1 else (1, 2, 3, None)
    return (slots[0:2], slots[2:4])


def _hgrn2_masks(ts):
    t_idx = _iota((ts, LANES), 0)
    lo_half = _iota((ts, LANES), 1) < HEAD_DIM
    tt = _iota((ts, ts), 0)
    ss = _iota((ts, ts), 1)
    col_masks, blk_masks = [], []
    for outer, inner in _LEVELS:
        pos = (t_idx % outer) // inner
        cols = []
        for ja, jb in _level_slots(inner):
            jsel = jnp.where(lo_half, ja, -1 if jb is None else jb)
            cols.append((pos == jsel, (pos <= jsel) if inner == 1 else (pos < jsel)))
        col_masks.append(cols)
        blk_masks.append(None if outer >= ts else (tt // outer) == (ss // outer))
    return lo_half, col_masks, blk_masks


def _hgrn2_head_att(qd, kd, bd, masks):
    lo_half, col_masks, blk_masks = masks
    att = None
    for (outer, inner), cols, blk in zip(_LEVELS, col_masks, blk_masks):
        qcols, kcols = [], []
        for (ja, jb), (qmask, kmask) in zip(_level_slots(inner), cols):
            ref = _ref_rows(bd, outer, inner, ja)
            if jb is not None:
                ref = jnp.where(lo_half, ref, _ref_rows(bd, outer, inner, jb))
            qcols.append(jnp.where(qmask, qd * jnp.exp2(bd - ref), 0.0))
            kcols.append(jnp.where(kmask, kd * jnp.exp2(ref - bd), 0.0))
        q_l = jnp.concatenate(qcols, axis=1).astype(BF16)
        k_l = jnp.concatenate(kcols, axis=1).astype(BF16)
        a_l = _dot_nt(q_l, k_l)
        if blk is not None:
            a_l = jnp.where(blk, a_l, 0.0)
        att = a_l if att is None else att + a_l
    return att


def _dup_heads(x):
    out = []
    lo_half = _iota((x.shape[0], LANES), 1) < HEAD_DIM
    for pair in range(N_HEADS // 2):
        xp = x[:, pair * LANES:(pair + 1) * LANES]
        rot = pltpu.roll(xp, HEAD_DIM, 1)
        out.append(jnp.where(lo_half, xp, rot))
        out.append(jnp.where(lo_half, rot, xp))
    return out


def _hgrn2(z_c, lb, norm_g, state_t):
    ts = z_c.shape[0]
    q, f, v, g = (z_c[:, n * W_GRP:(n + 1) * W_GRP] for n in range(4))
    qf = q * jax.nn.sigmoid(q)
    fgate = lb + (1.0 - lb) * jax.nn.sigmoid(f)
    log2f = jnp.log(fgate) * LOG2E
    kf = 1.0 - fgate
    tri = (_iota((ts, ts), 1) <= _iota((ts, ts), 0)).astype(BF16)
    p1, p2, p3 = _split3(log2f)
    bb = _dot(tri, p1) + _dot(tri, p2) + _dot(tri, p3)

    head = _iota((ts, W_GRP), 1) // HEAD_DIM
    v16 = v.astype(BF16)
    qh, kh, bh = _dup_heads(qf), _dup_heads(kf), _dup_heads(bb)
    masks = _hgrn2_masks(ts)
    yield
    o = _dot_nt((qf * jnp.exp2(bb)).astype(BF16), state_t[...].astype(BF16))
    for h in range(N_HEADS):
        att = _hgrn2_head_att(qh[h], kh[h], bh[h], masks)
        o = o + _dot(att.astype(BF16), jnp.where(head == h, v16, jnp.zeros_like(v16)))
        yield

    bl = bb[ts - 1:ts, :]
    k_hat = (kf * jnp.exp2(bl - bb)).astype(BF16)
    upd = _dot_tn(v16, k_hat)
    same_head = (_iota(upd.shape, 0) // HEAD_DIM) == (_iota(upd.shape, 1) // HEAD_DIM)
    state_t[...] = state_t[...] * jnp.exp2(bl) + jnp.where(same_head, upd, 0.0)

    avg = jnp.where(same_head, 1.0 / HEAD_DIM, 0.0).astype(BF16)
    oo = o * o
    o_hi = oo.astype(BF16)
    o_lo = (oo - o_hi.astype(F32)).astype(BF16)
    ms = _dot(o_hi, avg) + _dot(o_lo, avg)
    o = o * lax.rsqrt(ms + EPS) * norm_g
    return o * (g * jax.nn.sigmoid(g))


def _pool(xd, dw, scale, pool_slab, tile_start):
    ts = xd.shape[0]
    lo_half = _iota((ts, LANES), 1) < HEAD_DIM
    pos = (tile_start + _iota((ts, LANES), 0) + 1).astype(F32)
    pooled = []
    for c in range(W_GRP // LANES):
        slab = pool_slab.at[c]
        x_tile = xd[:, c * LANES:(c + 1) * LANES]
        slab[POOL_HIST:POOL_HIST + ts, :] = x_tile
        w_lo, w_hi = POOL_WINDOWS[2 * c], POOL_WINDOWS[2 * c + 1]
        if w_lo % SUBLANES == 0 and w_hi == 2 * w_lo:
            acc = slab[POOL_HIST - w_lo:POOL_HIST + ts, :]
            for k in range(1, w_lo):
                acc = acc + slab[POOL_HIST - w_lo - k:POOL_HIST + ts - k, :]
            s_lo = acc[w_lo:]
            s_hi = s_lo + acc[:ts]
        else:
            acc = x_tile
            for k in range(1, w_lo):
                acc = acc + slab[POOL_HIST - k:POOL_HIST - k + ts, :]
            s_lo = acc
            for k in range(w_lo, w_hi):
                acc = acc + slab[POOL_HIST - k:POOL_HIST - k + ts, :]
            s_hi = acc
        slab[0:POOL_HIST, :] = slab[ts:ts + POOL_HIST, :]
        wsum = jnp.where(lo_half, s_lo, s_hi)
        wlen = jnp.where(lo_half, float(w_lo), float(w_hi))
        pooled.append(wsum / jnp.minimum(pos, wlen) - x_tile)
    return _dot(jnp.concatenate(pooled, axis=1).astype(BF16), dw) * scale


def _mixer_steps(layer, tile_start, x, consts, scratch, out):
    (g1_ref, win_ref, alng_ref, alnb_ref, aws_ref, absx_ref, bcw_ref, bcb_ref, bwax_ref, bbax_ref, blam_ref,
     clb_ref, cng_ref, dw_ref, dsc_ref, wout_ref) = consts
    conv_slab, a_slab, b_slab, h_carry, state_t, pool_slab = scratch
    h = _rms_norm(x, g1_ref[...]).astype(BF16)
    off_b, off_c, off_d = 2 * W_GRP, 4 * W_GRP, 8 * W_GRP

    y_a = _gmlp(_dot(h, win_ref[:, 0:off_b]), alng_ref[...], alnb_ref[...], aws_ref[...], absx_ref[...])
    yield
    y_b = _rglru(_dot(h, win_ref[:, off_b:off_c]), bcw_ref[...], bcb_ref[...], bwax_ref[...], bbax_ref[...],
                 blam_ref[...], conv_slab, a_slab, b_slab, h_carry)
    yield
    clb = clb_ref[...]
    e = jnp.exp(clb - jnp.max(clb, axis=0, keepdims=True))
    sm = e / jnp.sum(e, axis=0, keepdims=True)
    lb = jnp.zeros((1, W_GRP), F32)
    for j in range(1, layer + 1):
        lb = lb + sm[j:j + 1, :]
    y_c = yield from _hgrn2(_dot(h, win_ref[:, off_c:off_d]), lb, cng_ref[...], state_t)
    yield
    y_d = _pool(_dot(h, win_ref[:, off_d:off_d + W_GRP]), dw_ref[...], dsc_ref[...], pool_slab, tile_start)
    yield
    mix = jnp.concatenate([y_a, y_b, y_c, y_d], axis=1).astype(BF16)
    out.append(x + _dot(mix, wout_ref[...]))


def _ffn_steps(final, x, p_tile, consts, hbuf, out):
    g2_ref, wup_ref, cw_ref, cb_ref, wdown_ref, g3_ref, wpe_ref, wpg_ref, gf_ref = consts
    d_ff = wdown_ref.shape[0]
    n_chunks = d_ff // FF_CHUNK
    h2 = _rms_norm(x, g2_ref[...]).astype(BF16)

    def up(col0):
        return _dot(h2, wup_ref[:, col0:col0 + FF_CHUNK])

    def conv(col0, hf):
        outs = []
        for c in range(FF_CHUNK // LANES):
            lanes = slice(col0 + c * LANES, col0 + (c + 1) * LANES)
            outs.append(_causal_conv_slab(hbuf.at[(col0 // LANES) + c], hf[:, c * LANES:(c + 1) * LANES],
                                          cw_ref[:, lanes], cb_ref[:, lanes]))
        return jnp.concatenate(outs, axis=1)

    acc = x
    hg_next, hv_next = up(0), up(d_ff)
    yield
    for j in range(n_chunks):
        hg, hv = hg_next, hv_next
        if j + 1 < n_chunks:
            hg_next, hv_next = up((j + 1) * FF_CHUNK), up(d_ff + (j + 1) * FF_CHUNK)
        gt = conv(j * FF_CHUNK, hg)
        val = conv(d_ff + j * FF_CHUNK, hv)
        act = (_gelu_tanh(gt) * val).astype(BF16)
        acc = acc + _dot(act, wdown_ref[j * FF_CHUNK:(j + 1) * FF_CHUNK, :])
        yield
    x = acc
    gate = jax.nn.sigmoid(_dot(_rms_norm(x, g3_ref[...]).astype(BF16), wpg_ref[...]))
    x = x + _dot(p_tile.astype(BF16), wpe_ref[...]) * gate
    if final:
        x = _rms_norm(x, gf_ref[...])
    out.append(x)


N_MIXER_CONSTS = 16
N_FFN_CONSTS = 9


def _layer_kernel(layer, final, n_tiles, x_ref, p_ref, *refs):
    mixer_consts = refs[:N_MIXER_CONSTS]
    ffn_consts = refs[N_MIXER_CONSTS:N_MIXER_CONSTS + N_FFN_CONSTS]
    o_ref = refs[N_MIXER_CONSTS + N_FFN_CONSTS]
    conv_slab, a_slab, b_slab, h_carry, state_t, pool_slab, hbuf, x1_buf = refs[N_MIXER_CONSTS + N_FFN_CONSTS + 1:]
    i = pl.program_id(0)
    s = i % n_tiles

    @pl.when(i == 0)
    def _():
        x1_buf[...] = jnp.zeros_like(x1_buf)

    @pl.when(s == 0)
    def _():
        conv_slab[:, 0:CONV_HIST, :] = jnp.zeros((conv_slab.shape[0], CONV_HIST, LANES), F32)
        pool_slab[:, 0:POOL_HIST, :] = jnp.zeros((pool_slab.shape[0], POOL_HIST, LANES), F32)
        h_carry[...] = jnp.zeros_like(h_carry)
        state_t[...] = jnp.zeros_like(state_t)

    @pl.when(jnp.maximum(i - 1, 0) % n_tiles == 0)
    def _():
        hbuf[:, 0:CONV_HIST, :] = jnp.zeros((hbuf.shape[0], CONV_HIST, LANES), F32)

    ts = x_ref.shape[0]
    mixer_out, ffn_out = [], []
    gens = [
        _ffn_steps(final, x1_buf[(i + 1) % 2], p_ref[...], ffn_consts, hbuf, ffn_out),
        _mixer_steps(layer, s * ts, x_ref[...], mixer_consts,
                     (conv_slab, a_slab, b_slab, h_carry, state_t, pool_slab), mixer_out),
    ]
    while gens:
        for g in list(gens):
            try:
                next(g)
            except StopIteration:
                gens.remove(g)
    x1_buf[i % 2] = mixer_out[0]
    o_ref[...] = ffn_out[0]


def _layer_spec(arr, layer):
    zeros = (0,) * (arr.ndim - 1)
    return pl.BlockSpec((None,) + arr.shape[1:], lambda i: (layer,) + zeros, pipeline_mode=pl.Buffered(1))


def _params():
    return pltpu.CompilerParams(dimension_semantics=("arbitrary",), vmem_limit_bytes=VMEM_LIMIT)


def _layer_call(layer, final, x, p, mixer_consts, ffn_consts):
    bsz, seq, d = x.shape
    n_tiles = seq // TS
    n_steps = bsz * n_tiles
    d_ff = ffn_consts[4].shape[1]
    consts = tuple(mixer_consts) + tuple(ffn_consts)
    assert len(mixer_consts) == N_MIXER_CONSTS and len(ffn_consts) == N_FFN_CONSTS

    def mixer_tile(i):
        t = jnp.minimum(i, n_steps - 1)
        return t // n_tiles, t % n_tiles

    def ffn_tile(i):
        t = jnp.maximum(i - 1, 0)
        return t // n_tiles, t % n_tiles

    return pl.pallas_call(
        functools.partial(_layer_kernel, layer, final, n_tiles),
        out_shape=jax.ShapeDtypeStruct(x.shape, x.dtype),
        grid=(n_steps + 1,),
        in_specs=[pl.BlockSpec((None, TS, d), lambda i: mixer_tile(i) + (0,)),
                  pl.BlockSpec((None, None, TS, p.shape[-1]), lambda i: (layer,) + ffn_tile(i) + (0,))]
        + [_layer_spec(c, layer) for c in consts],
        out_specs=pl.BlockSpec((None, TS, d), lambda i: ffn_tile(i) + (0,)),
        scratch_shapes=[
            pltpu.VMEM((W_GRP // LANES, CONV_HIST + TS, LANES), F32),
            pltpu.VMEM((W_GRP // LANES, SUBLANES * SCAN_PITCH, LANES), F32),
            pltpu.VMEM((W_GRP // LANES, SUBLANES * SCAN_PITCH, LANES), F32),
            pltpu.VMEM((SUBLANES, W_GRP), F32),
            pltpu.VMEM((W_GRP, W_GRP), F32),
            pltpu.VMEM((W_GRP // LANES, POOL_HIST + TS, LANES), F32),
            pltpu.VMEM((2 * d_ff // LANES, CONV_HIST + TS, LANES), F32),
            pltpu.VMEM((2, TS, d), F32),
        ],
        compiler_params=_params(),
        name=f"layer_l{layer}",
    )(x, p, *consts)


def _block_diag(w):
    depth, n_h, d_in, d_out = w.shape
    eye = jnp.eye(n_h, dtype=w.dtype)
    return jnp.einsum("lhde,hg->lhdge", w, eye).reshape(depth, n_h * d_in, n_h * d_out)


def _rows(v):
    return v.reshape(v.shape[0], 1, -1).astype(F32)


def kernel(x, p, norm1_g, w_in, a_ln_g, a_ln_b, a_ws, a_bs, b_conv_w, b_conv_b, b_wa, b_ba, b_wx, b_bx, b_lam, c_lb, c_norm_g, d_w, d_scale, w_out, norm2_g, w_up, ffn_conv_w, ffn_conv_b, w_down, norm3_g, w_pe, w_pg, final_g):
    depth = w_in.shape[0]
    d_ff = w_down.shape[1]
    assert x.shape[1] % TS == 0 and d_ff % FF_CHUNK == 0
    mixer_consts = (
        _rows(norm1_g),
        w_in.astype(BF16),
        _rows(a_ln_g), _rows(a_ln_b),
        a_ws.transpose(0, 2, 1, 3).reshape(depth, GMLP_CHUNK, N_HEADS * GMLP_CHUNK).astype(F32),
        jnp.repeat(a_bs.transpose(0, 2, 1), HEAD_DIM, axis=2).astype(F32),
        b_conv_w.astype(F32), _rows(b_conv_b),
        jnp.concatenate([_block_diag(b_wa), _block_diag(b_wx)], axis=2).astype(BF16),
        _rows(jnp.concatenate([b_ba, b_bx], axis=1)),
        _rows(b_lam),
        jnp.broadcast_to(c_lb.astype(F32), (depth,) + c_lb.shape),
        _rows(jnp.tile(c_norm_g, (1, N_HEADS))),
        _block_diag(d_w).astype(BF16), _rows(d_scale),
        w_out.astype(BF16),
    )
    ffn_consts = (
        _rows(norm2_g),
        w_up.astype(BF16),
        ffn_conv_w.astype(F32), _rows(ffn_conv_b),
        w_down.astype(BF16),
        _rows(norm3_g),
        w_pe.astype(BF16), w_pg.astype(BF16),
        jnp.broadcast_to(final_g.astype(F32).reshape(1, 1, -1), (depth, 1, final_g.shape[0])),
    )
    for l in range(depth):
        x = _layer_call(l, l == depth - 1, x, p, mixer_consts, ffn_consts)
    return x
```

```python
import functools

import jax
import jax.numpy as jnp
from jax import lax
from jax.experimental import pallas as pl
from jax.experimental.pallas import tpu as pltpu

F32 = jnp.float32
BF16 = jnp.bfloat16

EPS = 1e-6
N_HEADS = 4
HEAD_DIM = 64
W_GRP = N_HEADS * HEAD_DIM
GMLP_CHUNK = 128
RGLRU_C = 8.0
POOL_WINDOWS = (2, 4, 8, 16)
POOL_HIST = 16
CONV_HIST = 8
LANES = 128
SUBLANES = 8
TS = 256
SCAN_SEG = TS // SUBLANES
SCAN_PITCH = SCAN_SEG + 1
FF_CHUNK = 256
LOG2E = 1.4426950408889634
VMEM_LIMIT = 56 * 1024 * 1024


def _dot(a, b):
    return jnp.dot(a, b, preferred_element_type=F32)


def _dot_nt(a, b):
    return lax.dot_general(a, b, (((1,), (1,)), ((), ())), preferred_element_type=F32)


def _dot_tn(a, b):
    return lax.dot_general(a, b, (((0,), (0,)), ((), ())), preferred_element_type=F32)


def _rms_norm(x, g):
    return x * lax.rsqrt(jnp.mean(x * x, axis=-1, keepdims=True) + EPS) * g


def _iota(shape, axis):
    return lax.broadcasted_iota(jnp.int32, shape, axis)


def _gelu_tanh(x):
    c = 0.7978845608028654
    inner = x * (c + (c * 0.044715) * (x * x))
    return x * (0.5 + 0.5 * jnp.tanh(inner))


def _split3(x):
    p1 = x.astype(BF16)
    r1 = x - p1.astype(F32)
    p2 = r1.astype(BF16)
    p3 = (r1 - p2.astype(F32)).astype(BF16)
    return p1, p2, p3


def _init_constants(ws_cat, wm_s, tri_s, avg_s, blk_s, head_s):
    ts = tri_s.shape[0]
    t_idx = _iota(ws_cat.shape, 0)
    s_idx = _iota(ws_cat.shape, 1) % GMLP_CHUNK
    wm_s[...] = jnp.where(s_idx <= t_idx, ws_cat, 0.0).astype(BF16)
    tt = _iota((ts, ts), 0)
    ss = _iota((ts, ts), 1)
    tri_s[...] = (ss <= tt).astype(BF16)
    same_head = (tt // HEAD_DIM) == (ss // HEAD_DIM)
    avg_s[...] = jnp.where(same_head, 1.0 / HEAD_DIM, 0.0).astype(BF16)
    head_s[...] = jnp.where(same_head, 1.0, 0.0)
    for n, (outer, _) in enumerate(_LEVELS[1:]):
        blk_s[n] = jnp.where((tt // outer) == (ss // outer), 1.0, 0.0)


def _gmlp(z_a, ln_g, ln_b, wm, bs_x):
    ts = z_a.shape[0]
    ab = _gelu_tanh(z_a)
    u, v = ab[:, :W_GRP], ab[:, W_GRP:]
    mu = jnp.mean(v, axis=-1, keepdims=True)
    var = jnp.mean(jnp.square(v - mu), axis=-1, keepdims=True)
    vn = (v - mu) * lax.rsqrt(var + EPS) * ln_g + ln_b
    head = _iota((GMLP_CHUNK, W_GRP), 1) // HEAD_DIM
    outs = []
    for c in range(ts // GMLP_CHUNK):
        vc = vn[c * GMLP_CHUNK:(c + 1) * GMLP_CHUNK]
        rhs = jnp.concatenate([jnp.where(head == h, vc, 0.0) for h in range(N_HEADS)], axis=0).astype(BF16)
        sv = _dot(wm, rhs) + bs_x
        outs.append(u[c * GMLP_CHUNK:(c + 1) * GMLP_CHUNK] * sv)
    return jnp.concatenate(outs, axis=0)


def _causal_conv_slab(slab, x_tile, w, b):
    ts = x_tile.shape[0]
    k_width = w.shape[0]
    slab[CONV_HIST:CONV_HIST + ts, :] = x_tile
    y = b + w[k_width - 1:k_width] * x_tile
    for k in range(k_width - 1):
        shift = k_width - 1 - k
        y = y + w[k:k + 1] * slab[CONV_HIST - shift:CONV_HIST - shift + ts, :]
    slab[0:CONV_HIST, :] = slab[ts:ts + CONV_HIST, :]
    return y


def _rglru(z_b, conv_w, conv_b, wax, bax, lam, conv_slab, a_slab, b_slab, h_carry):
    ts = z_b.shape[0]
    n_tiles = W_GRP // LANES
    xb, gb = z_b[:, :W_GRP], z_b[:, W_GRP:]
    xc = jnp.concatenate(
        [_causal_conv_slab(conv_slab.at[c], xb[:, c * LANES:(c + 1) * LANES],
                           conv_w[:, c * LANES:(c + 1) * LANES], conv_b[:, c * LANES:(c + 1) * LANES])
         for c in range(n_tiles)], axis=1)
    ri = jax.nn.sigmoid(_dot(xc.astype(BF16), wax) + bax)
    r, i = ri[:, :W_GRP], ri[:, W_GRP:]
    log_a = (-RGLRU_C) * r * jax.nn.softplus(-lam)
    a = jnp.exp(log_a)
    th = jnp.tanh(log_a)
    mult = jnp.sqrt(-2.0 * th / (1.0 - th))
    bt = mult * (i * xc)
    assert ts == SUBLANES * SCAN_SEG
    hs = []
    for c in range(n_tiles):
        lanes = slice(c * LANES, (c + 1) * LANES)
        for sg in range(SUBLANES):
            rows = slice(sg * SCAN_SEG, (sg + 1) * SCAN_SEG)
            a_slab[c, sg * SCAN_PITCH:sg * SCAN_PITCH + SCAN_SEG, :] = a[rows, lanes]
            b_slab[c, sg * SCAN_PITCH:sg * SCAN_PITCH + SCAN_SEG, :] = bt[rows, lanes]
        a_cum = h_loc = None
        for j in range(SCAN_SEG):
            col = pl.ds(j, SUBLANES, stride=SCAN_PITCH)
            a_j, b_j = a_slab[c, col, :], b_slab[c, col, :]
            if j == 0:
                a_cum, h_loc = a_j, b_j
            else:
                h_loc = a_j * h_loc + b_j
                a_cum = a_j * a_cum
            a_slab[c, col, :] = a_cum
            b_slab[c, col, :] = h_loc
        carry = h_carry[0:1, lanes]
        segs = []
        for sg in range(SUBLANES):
            rows = slice(sg * SCAN_PITCH, sg * SCAN_PITCH + SCAN_SEG)
            segs.append(b_slab[c, rows, :] + a_slab[c, rows, :] * carry)
            carry = a_cum[sg:sg + 1] * carry + h_loc[sg:sg + 1]
        h_carry[:, lanes] = jnp.broadcast_to(carry, (h_carry.shape[0], LANES))
        hs.append(jnp.concatenate(segs, axis=0))
    h = jnp.concatenate(hs, axis=1)
    return h * _gelu_tanh(gb)


_LEVELS = ((256, 64), (64, 16), (16, 4), (4, 1))


def _level_slots(inner):
    slots = (0, 1, 2, 3) if inner == 1 else (1, 2, 3, None)
    return (slots[0:2], slots[2:4])


def _hgrn2_head_att(qd, kd, b_slab, blk_s):
    ts = qd.shape[0]
    lo1 = _iota((1, LANES), 1) < HEAD_DIM
    att = None
    for n_level, (outer, inner) in enumerate(_LEVELS[:3]):
        piece = max(inner, 2 * SUBLANES)
        lo = jnp.broadcast_to(lo1, (piece, LANES))
        pos = (_iota((piece, LANES), 0) % outer) // inner
        qcols, kcols = [], []
        for ja, jb in _level_slots(inner):
            jsel = jnp.where(lo, ja, -1 if jb is None else jb)
            qp, kp = [], []
            for base in range(0, ts, outer):
                ref = b_slab[base + ja * inner:base + ja * inner + 1, :]
                if jb is not None:
                    ref = jnp.where(lo1, ref, b_slab[base + jb * inner:base + jb * inner + 1, :])
                for r0 in range(base, base + outer, piece):
                    p0 = (r0 - base) // inner
                    p1 = (r0 - base + piece - 1) // inner
                    js = [j for j in (ja, jb) if j is not None]
                    rows = slice(r0, r0 + piece)
                    pos_r = pos + p0 if inner >= piece else pos
                    if any(p0 <= j <= p1 for j in js):
                        qp.append(jnp.where(pos_r == jsel, qd[rows] * jnp.exp2(b_slab[rows, :] - ref), 0.0))
                    else:
                        qp.append(jnp.zeros((piece, LANES), F32))
                    if any(p0 < j for j in js):
                        kp.append(jnp.where(pos_r < jsel, kd[rows] * jnp.exp2(ref - b_slab[rows, :]), 0.0))
                    else:
                        kp.append(jnp.zeros((piece, LANES), F32))
            qcols.append(jnp.concatenate(qp, axis=0))
            kcols.append(jnp.concatenate(kp, axis=0))
        a_l = _dot_nt(jnp.concatenate(qcols, axis=1).astype(BF16), jnp.concatenate(kcols, axis=1).astype(BF16))
        if outer < ts:
            a_l = a_l * blk_s[n_level - 1]
        att = a_l if att is None else att + a_l

    outer, inner = _LEVELS[3]
    assert inner == 1 and outer == 4
    lo = jnp.broadcast_to(lo1, (ts, LANES))
    m = _iota((ts, LANES), 0) % outer
    b0 = b_slab[0:ts, :]
    e = [None] + [jnp.exp2(b_slab[d:ts + d, :] - b0) for d in range(1, outer)]
    qcols, kcols = [], []
    for ja, jb in _level_slots(inner):
        jsel = jnp.where(lo, ja, jb)
        qcols.append(jnp.where(m == jsel, qd, 0.0))
        d = jsel - m
        f = jnp.where(d == 0, 1.0, 0.0)
        for dd in range(1, outer):
            f = jnp.where(d == dd, e[dd], f)
        kcols.append(kd * f)
    a_l = _dot_nt(jnp.concatenate(qcols, axis=1).astype(BF16), jnp.concatenate(kcols, axis=1).astype(BF16))
    return att + a_l * blk_s[len(_LEVELS) - 2]


def _dup_heads(x):
    out = []
    lo_half = _iota((x.shape[0], LANES), 1) < HEAD_DIM
    for pair in range(N_HEADS // 2):
        xp = x[:, pair * LANES:(pair + 1) * LANES]
        rot = pltpu.roll(xp, HEAD_DIM, 1)
        out.append(jnp.where(lo_half, xp, rot))
        out.append(jnp.where(lo_half, rot, xp))
    return out


def _hgrn2(z_c, lb, norm_g, state_t, b_slabs, tri_s, avg_s, blk_s, head_s):
    ts = z_c.shape[0]
    q, f, v, g = (z_c[:, n * W_GRP:(n + 1) * W_GRP] for n in range(4))
    qf = q * jax.nn.sigmoid(q)
    fgate = lb + (1.0 - lb) * jax.nn.sigmoid(f)
    log2f = jnp.log(fgate) * LOG2E
    kf = 1.0 - fgate
    tri = tri_s[...]
    p1, p2, p3 = _split3(log2f)
    bb = _dot(tri, p1) + _dot(tri, p2) + _dot(tri, p3)

    head = _iota((ts, W_GRP), 1) // HEAD_DIM
    v16 = v.astype(BF16)
    qh, kh = _dup_heads(qf), _dup_heads(kf)
    for h, bh in enumerate(_dup_heads(bb)):
        b_slabs[h, 0:ts, :] = bh
        b_slabs[h, ts:ts + SUBLANES, :] = jnp.zeros((SUBLANES, LANES), F32)
    yield
    o = _dot_nt((qf * jnp.exp2(bb)).astype(BF16), state_t[...].astype(BF16))
    for h in range(N_HEADS):
        att = _hgrn2_head_att(qh[h], kh[h], b_slabs.at[h], blk_s)
        o = o + _dot(att.astype(BF16), jnp.where(head == h, v16, jnp.zeros_like(v16)))
        yield

    bl = bb[ts - 1:ts, :]
    k_hat = (kf * jnp.exp2(bl - bb)).astype(BF16)
    upd = _dot_tn(v16, k_hat)
    state_t[...] = state_t[...] * jnp.exp2(bl) + upd * head_s[...]

    avg = avg_s[...]
    oo = o * o
    o_hi = oo.astype(BF16)
    o_lo = (oo - o_hi.astype(F32)).astype(BF16)
    ms = _dot(o_hi, avg) + _dot(o_lo, avg)
    o = o * lax.rsqrt(ms + EPS) * norm_g
    return o * (g * jax.nn.sigmoid(g))


def _pool(xd, dw, scale, pool_slab, tile_start):
    ts = xd.shape[0]
    lo_half = _iota((ts, LANES), 1) < HEAD_DIM
    pos = (tile_start + _iota((ts, LANES), 0) + 1).astype(F32)
    pooled = []
    for c in range(W_GRP // LANES):
        slab = pool_slab.at[c]
        x_tile = xd[:, c * LANES:(c + 1) * LANES]
        slab[POOL_HIST:POOL_HIST + ts, :] = x_tile
        w_lo, w_hi = POOL_WINDOWS[2 * c], POOL_WINDOWS[2 * c + 1]
        if w_lo % SUBLANES == 0 and w_hi == 2 * w_lo:
            acc = slab[POOL_HIST - w_lo:POOL_HIST + ts, :]
            for k in range(1, w_lo):
                acc = acc + slab[POOL_HIST - w_lo - k:POOL_HIST + ts - k, :]
            s_lo = acc[w_lo:]
            s_hi = s_lo + acc[:ts]
        else:
            acc = x_tile
            for k in range(1, w_lo):
                acc = acc + slab[POOL_HIST - k:POOL_HIST - k + ts, :]
            s_lo = acc
            for k in range(w_lo, w_hi):
                acc = acc + slab[POOL_HIST - k:POOL_HIST - k + ts, :]
            s_hi = acc
        slab[0:POOL_HIST, :] = slab[ts:ts + POOL_HIST, :]
        wsum = jnp.where(lo_half, s_lo, s_hi)
        wlen = jnp.where(lo_half, float(w_lo), float(w_hi))
        pooled.append(wsum / jnp.minimum(pos, wlen) - x_tile)
    return _dot(jnp.concatenate(pooled, axis=1).astype(BF16), dw) * scale


def _mixer_steps(layer, tile_start, x, consts, scratch, out):
    (g1_ref, win_ref, alng_ref, alnb_ref, aws_ref, absx_ref, bcw_ref, bcb_ref, bwax_ref, bbax_ref, blam_ref,
     clb_ref, cng_ref, dw_ref, dsc_ref, wout_ref) = consts
    conv_slab, a_slab, b_slab, h_carry, state_t, pool_slab, c_slabs, wm_s, tri_s, avg_s, blk_s, head_s = scratch
    h = _rms_norm(x, g1_ref[...]).astype(BF16)
    off_b, off_c, off_d = 2 * W_GRP, 4 * W_GRP, 8 * W_GRP

    y_a = _gmlp(_dot(h, win_ref[:, 0:off_b]), alng_ref[...], alnb_ref[...], wm_s[...], absx_ref[...])
    yield
    y_b = _rglru(_dot(h, win_ref[:, off_b:off_c]), bcw_ref[...], bcb_ref[...], bwax_ref[...], bbax_ref[...],
                 blam_ref[...], conv_slab, a_slab, b_slab, h_carry)
    yield
    clb = clb_ref[...]
    e = jnp.exp(clb - jnp.max(clb, axis=0, keepdims=True))
    sm = e / jnp.sum(e, axis=0, keepdims=True)
    lb = jnp.zeros((1, W_GRP), F32)
    for j in range(1, layer + 1):
        lb = lb + sm[j:j + 1, :]
    y_c = yield from _hgrn2(_dot(h, win_ref[:, off_c:off_d]), lb, cng_ref[...], state_t, c_slabs, tri_s, avg_s,
                            blk_s, head_s)
    yield
    y_d = _pool(_dot(h, win_ref[:, off_d:off_d + W_GRP]), dw_ref[...], dsc_ref[...], pool_slab, tile_start)
    yield
    mix = jnp.concatenate([y_a, y_b, y_c, y_d], axis=1).astype(BF16)
    out.append(x + _dot(mix, wout_ref[...]))


def _ffn_steps(final, x, p_tile, consts, hbuf, out):
    g2_ref, wup_ref, cw_ref, cb_ref, wdown_ref, g3_ref, wpe_ref, wpg_ref, gf_ref = consts
    d_ff = wdown_ref.shape[0]
    n_chunks = d_ff // FF_CHUNK
    h2 = _rms_norm(x, g2_ref[...]).astype(BF16)

    def up(col0):
        return _dot(h2, wup_ref[:, col0:col0 + FF_CHUNK])

    def conv(col0, hf):
        outs = []
        for c in range(FF_CHUNK // LANES):
            lanes = slice(col0 + c * LANES, col0 + (c + 1) * LANES)
            outs.append(_causal_conv_slab(hbuf.at[(col0 // LANES) + c], hf[:, c * LANES:(c + 1) * LANES],
                                          cw_ref[:, lanes], cb_ref[:, lanes]))
        return jnp.concatenate(outs, axis=1)

    acc = x
    hg_next, hv_next = up(0), up(d_ff)
    yield
    for j in range(n_chunks):
        hg, hv = hg_next, hv_next
        if j + 1 < n_chunks:
            hg_next, hv_next = up((j + 1) * FF_CHUNK), up(d_ff + (j + 1) * FF_CHUNK)
        gt = conv(j * FF_CHUNK, hg)
        val = conv(d_ff + j * FF_CHUNK, hv)
        act = (_gelu_tanh(gt) * val).astype(BF16)
        acc = acc + _dot(act, wdown_ref[j * FF_CHUNK:(j + 1) * FF_CHUNK, :])
        yield
    x = acc
    gate = jax.nn.sigmoid(_dot(_rms_norm(x, g3_ref[...]).astype(BF16), wpg_ref[...]))
    x = x + _dot(p_tile.astype(BF16), wpe_ref[...]) * gate
    if final:
        x = _rms_norm(x, gf_ref[...])
    out.append(x)


N_MIXER_CONSTS = 16
N_FFN_CONSTS = 9


def _layer_kernel(layer, final, n_tiles, x_ref, p_ref, *refs):
    mixer_consts = refs[:N_MIXER_CONSTS]
    ffn_consts = refs[N_MIXER_CONSTS:N_MIXER_CONSTS + N_FFN_CONSTS]
    o_ref = refs[N_MIXER_CONSTS + N_FFN_CONSTS]
    (conv_slab, a_slab, b_slab, h_carry, state_t, pool_slab, c_slabs, wm_s, tri_s, avg_s, blk_s, head_s, hbuf,
     x1_buf) = refs[N_MIXER_CONSTS + N_FFN_CONSTS + 1:]
    i = pl.program_id(0)
    s = i % n_tiles

    @pl.when(i == 0)
    def _():
        x1_buf[...] = jnp.zeros_like(x1_buf)
        _init_constants(mixer_consts[4][...], wm_s, tri_s, avg_s, blk_s, head_s)

    @pl.when(s == 0)
    def _():
        conv_slab[:, 0:CONV_HIST, :] = jnp.zeros((conv_slab.shape[0], CONV_HIST, LANES), F32)
        pool_slab[:, 0:POOL_HIST, :] = jnp.zeros((pool_slab.shape[0], POOL_HIST, LANES), F32)
        h_carry[...] = jnp.zeros_like(h_carry)
        state_t[...] = jnp.zeros_like(state_t)

    @pl.when(jnp.maximum(i - 1, 0) % n_tiles == 0)
    def _():
        hbuf[:, 0:CONV_HIST, :] = jnp.zeros((hbuf.shape[0], CONV_HIST, LANES), F32)

    ts = x_ref.shape[0]
    mixer_out, ffn_out = [], []
    gens = [
        _ffn_steps(final, x1_buf[(i + 1) % 2], p_ref[...], ffn_consts, hbuf, ffn_out),
        _mixer_steps(layer, s * ts, x_ref[...], mixer_consts,
                     (conv_slab, a_slab, b_slab, h_carry, state_t, pool_slab, c_slabs, wm_s, tri_s, avg_s, blk_s,
                      head_s), mixer_out),
    ]
    while gens:
        for g in list(gens):
            try:
                next(g)
            except StopIteration:
                gens.remove(g)
    x1_buf[i % 2] = mixer_out[0]
    o_ref[...] = ffn_out[0]


def _layer_spec(arr, layer):
    zeros = (0,) * (arr.ndim - 1)
    return pl.BlockSpec((None,) + arr.shape[1:], lambda i: (layer,) + zeros, pipeline_mode=pl.Buffered(1))


def _params():
    return pltpu.CompilerParams(dimension_semantics=("arbitrary",), vmem_limit_bytes=VMEM_LIMIT)


def _layer_call(layer, final, x, p, mixer_consts, ffn_consts):
    bsz, seq, d = x.shape
    n_tiles = seq // TS
    n_steps = bsz * n_tiles
    d_ff = ffn_consts[4].shape[1]
    consts = tuple(mixer_consts) + tuple(ffn_consts)
    assert len(mixer_consts) == N_MIXER_CONSTS and len(ffn_consts) == N_FFN_CONSTS

    def mixer_tile(i):
        t = jnp.minimum(i, n_steps - 1)
        return t // n_tiles, t % n_tiles

    def ffn_tile(i):
        t = jnp.maximum(i - 1, 0)
        return t // n_tiles, t % n_tiles

    return pl.pallas_call(
        functools.partial(_layer_kernel, layer, final, n_tiles),
        out_shape=jax.ShapeDtypeStruct(x.shape, x.dtype),
        grid=(n_steps + 1,),
        in_specs=[pl.BlockSpec((None, TS, d), lambda i: mixer_tile(i) + (0,)),
                  pl.BlockSpec((None, None, TS, p.shape[-1]), lambda i: (layer,) + ffn_tile(i) + (0,))]
        + [_layer_spec(c, layer) for c in consts],
        out_specs=pl.BlockSpec((None, TS, d), lambda i: ffn_tile(i) + (0,)),
        scratch_shapes=[
            pltpu.VMEM((W_GRP // LANES, CONV_HIST + TS, LANES), F32),
            pltpu.VMEM((W_GRP // LANES, SUBLANES * SCAN_PITCH, LANES), F32),
            pltpu.VMEM((W_GRP // LANES, SUBLANES * SCAN_PITCH, LANES), F32),
            pltpu.VMEM((SUBLANES, W_GRP), F32),
            pltpu.VMEM((W_GRP, W_GRP), F32),
            pltpu.VMEM((W_GRP // LANES, POOL_HIST + TS, LANES), F32),
            pltpu.VMEM((N_HEADS, TS + SUBLANES, LANES), F32),
            pltpu.VMEM((GMLP_CHUNK, N_HEADS * GMLP_CHUNK), BF16),
            pltpu.VMEM((TS, TS), BF16),
            pltpu.VMEM((W_GRP, W_GRP), BF16),
            pltpu.VMEM((len(_LEVELS) - 1, TS, TS), F32),
            pltpu.VMEM((W_GRP, W_GRP), F32),
            pltpu.VMEM((2 * d_ff // LANES, CONV_HIST + TS, LANES), F32),
            pltpu.VMEM((2, TS, d), F32),
        ],
        compiler_params=_params(),
        name=f"layer_l{layer}",
    )(x, p, *consts)


def _block_diag(w):
    depth, n_h, d_in, d_out = w.shape
    eye = jnp.eye(n_h, dtype=w.dtype)
    return jnp.einsum("lhde,hg->lhdge", w, eye).reshape(depth, n_h * d_in, n_h * d_out)


def _rows(v):
    return v.reshape(v.shape[0], 1, -1).astype(F32)


def kernel(x, p, norm1_g, w_in, a_ln_g, a_ln_b, a_ws, a_bs, b_conv_w, b_conv_b, b_wa, b_ba, b_wx, b_bx, b_lam, c_lb, c_norm_g, d_w, d_scale, w_out, norm2_g, w_up, ffn_conv_w, ffn_conv_b, w_down, norm3_g, w_pe, w_pg, final_g):
    depth = w_in.shape[0]
    d_ff = w_down.shape[1]
    assert x.shape[1] % TS == 0 and d_ff % FF_CHUNK == 0
    mixer_consts = (
        _rows(norm1_g),
        w_in.astype(BF16),
        _rows(a_ln_g), _rows(a_ln_b),
        a_ws.transpose(0, 2, 1, 3).reshape(depth, GMLP_CHUNK, N_HEADS * GMLP_CHUNK).astype(F32),
        jnp.repeat(a_bs.transpose(0, 2, 1), HEAD_DIM, axis=2).astype(F32),
        b_conv_w.astype(F32), _rows(b_conv_b),
        jnp.concatenate([_block_diag(b_wa), _block_diag(b_wx)], axis=2).astype(BF16),
        _rows(jnp.concatenate([b_ba, b_bx], axis=1)),
        _rows(b_lam),
        jnp.broadcast_to(c_lb.astype(F32), (depth,) + c_lb.shape),
        _rows(jnp.tile(c_norm_g, (1, N_HEADS))),
        _block_diag(d_w).astype(BF16), _rows(d_scale),
        w_out.astype(BF16),
    )
    ffn_consts = (
        _rows(norm2_g),
        w_up.astype(BF16),
        ffn_conv_w.astype(F32), _rows(ffn_conv_b),
        w_down.astype(BF16),
        _rows(norm3_g),
        w_pe.astype(BF16), w_pg.astype(BF16),
        jnp.broadcast_to(final_g.astype(F32).reshape(1, 1, -1), (depth, 1, final_g.shape[0])),
    )
    for l in range(depth):
        x = _layer_call(l, l == depth - 1, x, p, mixer_consts, ffn_consts)
    return x
```

```python
import functools

import jax
import jax.numpy as jnp
from jax import lax
from jax.experimental import pallas as pl
from jax.experimental.pallas import tpu as pltpu

F32 = jnp.float32
BF16 = jnp.bfloat16

EPS = 1e-6
N_HEADS = 4
HEAD_DIM = 64
W_GRP = N_HEADS * HEAD_DIM
GMLP_CHUNK = 128
RGLRU_C = 8.0
POOL_WINDOWS = (2, 4, 8, 16)
POOL_HIST = 16
CONV_HIST = 8
LANES = 128
SUBLANES = 8
TS = 256
SCAN_SEG = TS // SUBLANES
SCAN_PITCH = SCAN_SEG + 1
FF_CHUNK = 256
LOG2E = 1.4426950408889634
VMEM_LIMIT = 56 * 1024 * 1024


def _dot(a, b):
    return jnp.dot(a, b, preferred_element_type=F32)


def _dot_nt(a, b):
    return lax.dot_general(a, b, (((1,), (1,)), ((), ())), preferred_element_type=F32)


def _dot_tn(a, b):
    return lax.dot_general(a, b, (((0,), (0,)), ((), ())), preferred_element_type=F32)


def _rms_norm(x, g):
    return x * lax.rsqrt(jnp.mean(x * x, axis=-1, keepdims=True) + EPS) * g


def _iota(shape, axis):
    return lax.broadcasted_iota(jnp.int32, shape, axis)


def _gelu_tanh(x):
    c = 0.7978845608028654
    inner = x * (c + (c * 0.044715) * (x * x))
    return x * (0.5 + 0.5 * jnp.tanh(inner))


def _split3(x):
    p1 = x.astype(BF16)
    r1 = x - p1.astype(F32)
    p2 = r1.astype(BF16)
    p3 = (r1 - p2.astype(F32)).astype(BF16)
    return p1, p2, p3


def _init_constants(ws_cat, wm_s, tri_s, avg_s, blk_s, head_s):
    ts = tri_s.shape[0]
    t_idx = _iota(ws_cat.shape, 0)
    s_idx = _iota(ws_cat.shape, 1) % GMLP_CHUNK
    wm_s[...] = jnp.where(s_idx <= t_idx, ws_cat, 0.0).astype(BF16)
    tt = _iota((ts, ts), 0)
    ss = _iota((ts, ts), 1)
    tri_s[...] = (ss <= tt).astype(BF16)
    same_head = (tt // HEAD_DIM) == (ss // HEAD_DIM)
    avg_s[...] = jnp.where(same_head, 1.0 / HEAD_DIM, 0.0).astype(BF16)
    head_s[...] = jnp.where(same_head, 1.0, 0.0)
    for n, (outer, _) in enumerate(_LEVELS[1:]):
        blk_s[n] = jnp.where((tt // outer) == (ss // outer), 1.0, 0.0)


def _gmlp(z_a, ln_g, ln_b, wm, bs_x):
    ts = z_a.shape[0]
    ab = _gelu_tanh(z_a)
    u, v = ab[:, :W_GRP], ab[:, W_GRP:]
    mu = jnp.mean(v, axis=-1, keepdims=True)
    var = jnp.mean(jnp.square(v - mu), axis=-1, keepdims=True)
    rstd = lax.rsqrt(var + EPS)
    yield
    vn = (v - mu) * rstd * ln_g + ln_b
    head = _iota((GMLP_CHUNK, W_GRP), 1) // HEAD_DIM
    outs = []
    for c in range(ts // GMLP_CHUNK):
        vc = vn[c * GMLP_CHUNK:(c + 1) * GMLP_CHUNK]
        rhs = jnp.concatenate([jnp.where(head == h, vc, 0.0) for h in range(N_HEADS)], axis=0).astype(BF16)
        sv = _dot(wm, rhs) + bs_x
        outs.append(u[c * GMLP_CHUNK:(c + 1) * GMLP_CHUNK] * sv)
    return jnp.concatenate(outs, axis=0)


def _causal_conv_slab(slab, x_tile, w, b):
    ts = x_tile.shape[0]
    k_width = w.shape[0]
    slab[CONV_HIST:CONV_HIST + ts, :] = x_tile
    y = b + w[k_width - 1:k_width] * x_tile
    for k in range(k_width - 1):
        shift = k_width - 1 - k
        y = y + w[k:k + 1] * slab[CONV_HIST - shift:CONV_HIST - shift + ts, :]
    slab[0:CONV_HIST, :] = slab[ts:ts + CONV_HIST, :]
    return y


def _rglru(z_b, conv_w, conv_b, wax, bax, lam, conv_slab, a_slab, b_slab, h_carry):
    ts = z_b.shape[0]
    n_tiles = W_GRP // LANES
    xb, gb = z_b[:, :W_GRP], z_b[:, W_GRP:]
    xc = jnp.concatenate(
        [_causal_conv_slab(conv_slab.at[c], xb[:, c * LANES:(c + 1) * LANES],
                           conv_w[:, c * LANES:(c + 1) * LANES], conv_b[:, c * LANES:(c + 1) * LANES])
         for c in range(n_tiles)], axis=1)
    ri = jax.nn.sigmoid(_dot(xc.astype(BF16), wax) + bax)
    r, i = ri[:, :W_GRP], ri[:, W_GRP:]
    log_a = (-RGLRU_C) * r * jax.nn.softplus(-lam)
    a = jnp.exp(log_a)
    th = jnp.tanh(log_a)
    mult = jnp.sqrt(-2.0 * th / (1.0 - th))
    bt = mult * (i * xc)
    assert ts == SUBLANES * SCAN_SEG
    hs = []
    for c in range(n_tiles):
        lanes = slice(c * LANES, (c + 1) * LANES)
        for sg in range(SUBLANES):
            rows = slice(sg * SCAN_SEG, (sg + 1) * SCAN_SEG)
            a_slab[c, sg * SCAN_PITCH:sg * SCAN_PITCH + SCAN_SEG, :] = a[rows, lanes]
            b_slab[c, sg * SCAN_PITCH:sg * SCAN_PITCH + SCAN_SEG, :] = bt[rows, lanes]
        a_cum = h_loc = None
        for j in range(SCAN_SEG):
            col = pl.ds(j, SUBLANES, stride=SCAN_PITCH)
            a_j, b_j = a_slab[c, col, :], b_slab[c, col, :]
            if j == 0:
                a_cum, h_loc = a_j, b_j
            else:
                h_loc = a_j * h_loc + b_j
                a_cum = a_j * a_cum
            a_slab[c, col, :] = a_cum
            b_slab[c, col, :] = h_loc
        carry = h_carry[0:1, lanes]
        segs = []
        for sg in range(SUBLANES):
            rows = slice(sg * SCAN_PITCH, sg * SCAN_PITCH + SCAN_SEG)
            segs.append(b_slab[c, rows, :] + a_slab[c, rows, :] * carry)
            carry = a_cum[sg:sg + 1] * carry + h_loc[sg:sg + 1]
        h_carry[:, lanes] = jnp.broadcast_to(carry, (h_carry.shape[0], LANES))
        hs.append(jnp.concatenate(segs, axis=0))
    h = jnp.concatenate(hs, axis=1)
    return h * _gelu_tanh(gb)


_LEVELS = ((256, 64), (64, 16), (16, 4), (4, 1))


def _level_slots(inner):
    slots = (0, 1, 2, 3) if inner == 1 else (1, 2, 3, None)
    return (slots[0:2], slots[2:4])


def _hgrn2_head_att(qd, kd, b_slab, blk_s):
    ts = qd.shape[0]
    lo1 = _iota((1, LANES), 1) < HEAD_DIM
    att = None
    for n_level, (outer, inner) in enumerate(_LEVELS[:3]):
        piece = max(inner, 2 * SUBLANES)
        lo = jnp.broadcast_to(lo1, (piece, LANES))
        pos = (_iota((piece, LANES), 0) % outer) // inner
        qcols, kcols = [], []
        for ja, jb in _level_slots(inner):
            jsel = jnp.where(lo, ja, -1 if jb is None else jb)
            qp, kp = [], []
            for base in range(0, ts, outer):
                ref = b_slab[base + ja * inner:base + ja * inner + 1, :]
                if jb is not None:
                    ref = jnp.where(lo1, ref, b_slab[base + jb * inner:base + jb * inner + 1, :])
                for r0 in range(base, base + outer, piece):
                    p0 = (r0 - base) // inner
                    p1 = (r0 - base + piece - 1) // inner
                    js = [j for j in (ja, jb) if j is not None]
                    rows = slice(r0, r0 + piece)
                    pos_r = pos + p0 if inner >= piece else pos
                    if any(p0 <= j <= p1 for j in js):
                        qp.append(jnp.where(pos_r == jsel, qd[rows] * jnp.exp2(b_slab[rows, :] - ref), 0.0))
                    else:
                        qp.append(jnp.zeros((piece, LANES), F32))
                    if any(p0 < j for j in js):
                        kp.append(jnp.where(pos_r < jsel, kd[rows] * jnp.exp2(ref - b_slab[rows, :]), 0.0))
                    else:
                        kp.append(jnp.zeros((piece, LANES), F32))
            qcols.append(jnp.concatenate(qp, axis=0))
            kcols.append(jnp.concatenate(kp, axis=0))
        a_l = _dot_nt(jnp.concatenate(qcols, axis=1).astype(BF16), jnp.concatenate(kcols, axis=1).astype(BF16))
        if outer < ts:
            a_l = a_l * blk_s[n_level - 1]
        att = a_l if att is None else att + a_l

    outer, inner = _LEVELS[3]
    assert inner == 1 and outer == 4
    lo = jnp.broadcast_to(lo1, (ts, LANES))
    m = _iota((ts, LANES), 0) % outer
    b0 = b_slab[0:ts, :]
    e = [None] + [jnp.exp2(b_slab[d:ts + d, :] - b0) for d in range(1, outer)]
    qcols, kcols = [], []
    for ja, jb in _level_slots(inner):
        jsel = jnp.where(lo, ja, jb)
        qcols.append(jnp.where(m == jsel, qd, 0.0))
        d = jsel - m
        f = jnp.where(d == 0, 1.0, 0.0)
        for dd in range(1, outer):
            f = jnp.where(d == dd, e[dd], f)
        kcols.append(kd * f)
    a_l = _dot_nt(jnp.concatenate(qcols, axis=1).astype(BF16), jnp.concatenate(kcols, axis=1).astype(BF16))
    return att + a_l * blk_s[len(_LEVELS) - 2]


def _dup_heads(x):
    out = []
    lo_half = _iota((x.shape[0], LANES), 1) < HEAD_DIM
    for pair in range(N_HEADS // 2):
        xp = x[:, pair * LANES:(pair + 1) * LANES]
        rot = pltpu.roll(xp, HEAD_DIM, 1)
        out.append(jnp.where(lo_half, xp, rot))
        out.append(jnp.where(lo_half, rot, xp))
    return out


def _hgrn2(z_c, lb, norm_g, state_t, b_slabs, tri_s, avg_s, blk_s, head_s):
    ts = z_c.shape[0]
    q, f, v, g = (z_c[:, n * W_GRP:(n + 1) * W_GRP] for n in range(4))
    qf = q * jax.nn.sigmoid(q)
    fgate = lb + (1.0 - lb) * jax.nn.sigmoid(f)
    log2f = jnp.log(fgate) * LOG2E
    kf = 1.0 - fgate
    tri = tri_s[...]
    p1, p2, p3 = _split3(log2f)
    bb = _dot(tri, p1) + _dot(tri, p2) + _dot(tri, p3)
    yield

    head = _iota((ts, W_GRP), 1) // HEAD_DIM
    v16 = v.astype(BF16)
    qh, kh = _dup_heads(qf), _dup_heads(kf)
    for h, bh in enumerate(_dup_heads(bb)):
        b_slabs[h, 0:ts, :] = bh
        b_slabs[h, ts:ts + SUBLANES, :] = jnp.zeros((SUBLANES, LANES), F32)
    yield
    o = _dot_nt((qf * jnp.exp2(bb)).astype(BF16), state_t[...].astype(BF16))
    for h in range(N_HEADS):
        att = _hgrn2_head_att(qh[h], kh[h], b_slabs.at[h], blk_s)
        o = o + _dot(att.astype(BF16), jnp.where(head == h, v16, jnp.zeros_like(v16)))
        yield

    bl = bb[ts - 1:ts, :]
    k_hat = (kf * jnp.exp2(bl - bb)).astype(BF16)
    upd = _dot_tn(v16, k_hat)
    state_t[...] = state_t[...] * jnp.exp2(bl) + upd * head_s[...]

    avg = avg_s[...]
    oo = o * o
    o_hi = oo.astype(BF16)
    o_lo = (oo - o_hi.astype(F32)).astype(BF16)
    ms = _dot(o_hi, avg) + _dot(o_lo, avg)
    o = o * lax.rsqrt(ms + EPS) * norm_g
    return o * (g * jax.nn.sigmoid(g))


def _pool(xd, dw, scale, pool_slab, tile_start):
    ts = xd.shape[0]
    lo_half = _iota((ts, LANES), 1) < HEAD_DIM
    pos = (tile_start + _iota((ts, LANES), 0) + 1).astype(F32)
    pooled = []
    for c in range(W_GRP // LANES):
        slab = pool_slab.at[c]
        x_tile = xd[:, c * LANES:(c + 1) * LANES]
        slab[POOL_HIST:POOL_HIST + ts, :] = x_tile
        w_lo, w_hi = POOL_WINDOWS[2 * c], POOL_WINDOWS[2 * c + 1]
        if w_lo % SUBLANES == 0 and w_hi == 2 * w_lo:
            acc = slab[POOL_HIST - w_lo:POOL_HIST + ts, :]
            for k in range(1, w_lo):
                acc = acc + slab[POOL_HIST - w_lo - k:POOL_HIST + ts - k, :]
            s_lo = acc[w_lo:]
            s_hi = s_lo + acc[:ts]
        else:
            acc = x_tile
            for k in range(1, w_lo):
                acc = acc + slab[POOL_HIST - k:POOL_HIST - k + ts, :]
            s_lo = acc
            for k in range(w_lo, w_hi):
                acc = acc + slab[POOL_HIST - k:POOL_HIST - k + ts, :]
            s_hi = acc
        slab[0:POOL_HIST, :] = slab[ts:ts + POOL_HIST, :]
        wsum = jnp.where(lo_half, s_lo, s_hi)
        wlen = jnp.where(lo_half, float(w_lo), float(w_hi))
        pooled.append(wsum / jnp.minimum(pos, wlen) - x_tile)
    return _dot(jnp.concatenate(pooled, axis=1).astype(BF16), dw) * scale


def _mixer_steps(layer, tile_start, x, consts, scratch, out):
    (g1_ref, win_ref, alng_ref, alnb_ref, aws_ref, absx_ref, bcw_ref, bcb_ref, bwax_ref, bbax_ref, blam_ref,
     clb_ref, cng_ref, dw_ref, dsc_ref, wout_ref) = consts
    conv_slab, a_slab, b_slab, h_carry, state_t, pool_slab, c_slabs, wm_s, tri_s, avg_s, blk_s, head_s = scratch
    h = _rms_norm(x, g1_ref[...]).astype(BF16)
    off_b, off_c, off_d = 2 * W_GRP, 4 * W_GRP, 8 * W_GRP

    y_a = yield from _gmlp(_dot(h, win_ref[:, 0:off_b]), alng_ref[...], alnb_ref[...], wm_s[...], absx_ref[...])
    yield
    y_b = _rglru(_dot(h, win_ref[:, off_b:off_c]), bcw_ref[...], bcb_ref[...], bwax_ref[...], bbax_ref[...],
                 blam_ref[...], conv_slab, a_slab, b_slab, h_carry)
    yield
    clb = clb_ref[...]
    e = jnp.exp(clb - jnp.max(clb, axis=0, keepdims=True))
    sm = e / jnp.sum(e, axis=0, keepdims=True)
    lb = jnp.zeros((1, W_GRP), F32)
    for j in range(1, layer + 1):
        lb = lb + sm[j:j + 1, :]
    y_c = yield from _hgrn2(_dot(h, win_ref[:, off_c:off_d]), lb, cng_ref[...], state_t, c_slabs, tri_s, avg_s,
                            blk_s, head_s)
    yield
    y_d = _pool(_dot(h, win_ref[:, off_d:off_d + W_GRP]), dw_ref[...], dsc_ref[...], pool_slab, tile_start)
    yield
    mix = jnp.concatenate([y_a, y_b, y_c, y_d], axis=1).astype(BF16)
    out.append(x + _dot(mix, wout_ref[...]))


def _ffn_steps(final, x, p_tile, consts, hbuf, out):
    g2_ref, wup_ref, cw_ref, cb_ref, wdown_ref, g3_ref, wpe_ref, wpg_ref, gf_ref = consts
    d_ff = wdown_ref.shape[0]
    n_chunks = d_ff // FF_CHUNK
    h2 = _rms_norm(x, g2_ref[...]).astype(BF16)

    def up(col0):
        return _dot(h2, wup_ref[:, col0:col0 + FF_CHUNK])

    def conv(col0, hf):
        outs = []
        for c in range(FF_CHUNK // LANES):
            lanes = slice(col0 + c * LANES, col0 + (c + 1) * LANES)
            outs.append(_causal_conv_slab(hbuf.at[(col0 // LANES) + c], hf[:, c * LANES:(c + 1) * LANES],
                                          cw_ref[:, lanes], cb_ref[:, lanes]))
        return jnp.concatenate(outs, axis=1)

    acc = x
    hg_next, hv_next = up(0), up(d_ff)
    yield
    for j in range(n_chunks):
        hg, hv = hg_next, hv_next
        if j + 1 < n_chunks:
            hg_next, hv_next = up((j + 1) * FF_CHUNK), up(d_ff + (j + 1) * FF_CHUNK)
        gt = conv(j * FF_CHUNK, hg)
        val = conv(d_ff + j * FF_CHUNK, hv)
        act = (_gelu_tanh(gt) * val).astype(BF16)
        acc = acc + _dot(act, wdown_ref[j * FF_CHUNK:(j + 1) * FF_CHUNK, :])
        yield
    x = acc
    rstd = lax.rsqrt(jnp.mean(x * x, axis=-1, keepdims=True) + EPS)
    pe = _dot(p_tile.astype(BF16), wpe_ref[...])
    yield
    gate = jax.nn.sigmoid(_dot((x * rstd * g3_ref[...]).astype(BF16), wpg_ref[...]))
    x = x + pe * gate
    if final:
        x = _rms_norm(x, gf_ref[...])
    out.append(x)


N_MIXER_CONSTS = 16
N_FFN_CONSTS = 9


def _layer_kernel(layer, final, n_tiles, x_ref, p_ref, *refs):
    mixer_consts = refs[:N_MIXER_CONSTS]
    ffn_consts = refs[N_MIXER_CONSTS:N_MIXER_CONSTS + N_FFN_CONSTS]
    o_ref = refs[N_MIXER_CONSTS + N_FFN_CONSTS]
    (conv_slab, a_slab, b_slab, h_carry, state_t, pool_slab, c_slabs, wm_s, tri_s, avg_s, blk_s, head_s, hbuf,
     x1_buf) = refs[N_MIXER_CONSTS + N_FFN_CONSTS + 1:]
    i = pl.program_id(0)
    s = i % n_tiles

    @pl.when(i == 0)
    def _():
        x1_buf[...] = jnp.zeros_like(x1_buf)
        _init_constants(mixer_consts[4][...], wm_s, tri_s, avg_s, blk_s, head_s)

    @pl.when(s == 0)
    def _():
        conv_slab[:, 0:CONV_HIST, :] = jnp.zeros((conv_slab.shape[0], CONV_HIST, LANES), F32)
        pool_slab[:, 0:POOL_HIST, :] = jnp.zeros((pool_slab.shape[0], POOL_HIST, LANES), F32)
        h_carry[...] = jnp.zeros_like(h_carry)
        state_t[...] = jnp.zeros_like(state_t)

    @pl.when(jnp.maximum(i - 1, 0) % n_tiles == 0)
    def _():
        hbuf[:, 0:CONV_HIST, :] = jnp.zeros((hbuf.shape[0], CONV_HIST, LANES), F32)

    ts = x_ref.shape[0]
    mixer_out, ffn_out = [], []
    gens = [
        _ffn_steps(final, x1_buf[(i + 1) % 2], p_ref[...], ffn_consts, hbuf, ffn_out),
        _mixer_steps(layer, s * ts, x_ref[...], mixer_consts,
                     (conv_slab, a_slab, b_slab, h_carry, state_t, pool_slab, c_slabs, wm_s, tri_s, avg_s, blk_s,
                      head_s), mixer_out),
    ]
    ffn_gen, mixer_gen = gens
    order = [ffn_gen, mixer_gen] * 11 + [ffn_gen, ffn_gen, mixer_gen, ffn_gen]
    for g in order + [ffn_gen, mixer_gen] * 4:
        try:
            next(g)
        except StopIteration:
            pass
    assert mixer_out and ffn_out
    x1_buf[i % 2] = mixer_out[0]
    o_ref[...] = ffn_out[0]


def _layer_spec(arr, layer):
    zeros = (0,) * (arr.ndim - 1)
    return pl.BlockSpec((None,) + arr.shape[1:], lambda i: (layer,) + zeros, pipeline_mode=pl.Buffered(1))


def _params():
    return pltpu.CompilerParams(dimension_semantics=("arbitrary",), vmem_limit_bytes=VMEM_LIMIT)


def _layer_call(layer, final, x, p, mixer_consts, ffn_consts):
    bsz, seq, d = x.shape
    n_tiles = seq // TS
    n_steps = bsz * n_tiles
    d_ff = ffn_consts[4].shape[1]
    consts = tuple(mixer_consts) + tuple(ffn_consts)
    assert len(mixer_consts) == N_MIXER_CONSTS and len(ffn_consts) == N_FFN_CONSTS

    def mixer_tile(i):
        t = jnp.minimum(i, n_steps - 1)
        return t // n_tiles, t % n_tiles

    def ffn_tile(i):
        t = jnp.maximum(i - 1, 0)
        return t // n_tiles, t % n_tiles

    return pl.pallas_call(
        functools.partial(_layer_kernel, layer, final, n_tiles),
        out_shape=jax.ShapeDtypeStruct(x.shape, x.dtype),
        grid=(n_steps + 1,),
        in_specs=[pl.BlockSpec((None, TS, d), lambda i: mixer_tile(i) + (0,)),
                  pl.BlockSpec((None, None, TS, p.shape[-1]), lambda i: (layer,) + ffn_tile(i) + (0,))]
        + [_layer_spec(c, layer) for c in consts],
        out_specs=pl.BlockSpec((None, TS, d), lambda i: ffn_tile(i) + (0,)),
        scratch_shapes=[
            pltpu.VMEM((W_GRP // LANES, CONV_HIST + TS, LANES), F32),
            pltpu.VMEM((W_GRP // LANES, SUBLANES * SCAN_PITCH, LANES), F32),
            pltpu.VMEM((W_GRP // LANES, SUBLANES * SCAN_PITCH, LANES), F32),
            pltpu.VMEM((SUBLANES, W_GRP), F32),
            pltpu.VMEM((W_GRP, W_GRP), F32),
            pltpu.VMEM((W_GRP // LANES, POOL_HIST + TS, LANES), F32),
            pltpu.VMEM((N_HEADS, TS + SUBLANES, LANES), F32),
            pltpu.VMEM((GMLP_CHUNK, N_HEADS * GMLP_CHUNK), BF16),
            pltpu.VMEM((TS, TS), BF16),
            pltpu.VMEM((W_GRP, W_GRP), BF16),
            pltpu.VMEM((len(_LEVELS) - 1, TS, TS), F32),
            pltpu.VMEM((W_GRP, W_GRP), F32),
            pltpu.VMEM((2 * d_ff // LANES, CONV_HIST + TS, LANES), F32),
            pltpu.VMEM((2, TS, d), F32),
        ],
        compiler_params=_params(),
        name=f"layer_l{layer}",
    )(x, p, *consts)


def _block_diag(w):
    depth, n_h, d_in, d_out = w.shape
    eye = jnp.eye(n_h, dtype=w.dtype)
    return jnp.einsum("lhde,hg->lhdge", w, eye).reshape(depth, n_h * d_in, n_h * d_out)


def _rows(v):
    return v.reshape(v.shape[0], 1, -1).astype(F32)


def kernel(x, p, norm1_g, w_in, a_ln_g, a_ln_b, a_ws, a_bs, b_conv_w, b_conv_b, b_wa, b_ba, b_wx, b_bx, b_lam, c_lb, c_norm_g, d_w, d_scale, w_out, norm2_g, w_up, ffn_conv_w, ffn_conv_b, w_down, norm3_g, w_pe, w_pg, final_g):
    depth = w_in.shape[0]
    d_ff = w_down.shape[1]
    assert x.shape[1] % TS == 0 and d_ff % FF_CHUNK == 0
    mixer_consts = (
        _rows(norm1_g),
        w_in.astype(BF16),
        _rows(a_ln_g), _rows(a_ln_b),
        a_ws.transpose(0, 2, 1, 3).reshape(depth, GMLP_CHUNK, N_HEADS * GMLP_CHUNK).astype(F32),
        jnp.repeat(a_bs.transpose(0, 2, 1), HEAD_DIM, axis=2).astype(F32),
        b_conv_w.astype(F32), _rows(b_conv_b),
        jnp.concatenate([_block_diag(b_wa), _block_diag(b_wx)], axis=2).astype(BF16),
        _rows(jnp.concatenate([b_ba, b_bx], axis=1)),
        _rows(b_lam),
        jnp.broadcast_to(c_lb.astype(F32), (depth,) + c_lb.shape),
        _rows(jnp.tile(c_norm_g, (1, N_HEADS))),
        _block_diag(d_w).astype(BF16), _rows(d_scale),
        w_out.astype(BF16),
    )
    ffn_consts = (
        _rows(norm2_g),
        w_up.astype(BF16),
        ffn_conv_w.astype(F32), _rows(ffn_conv_b),
        w_down.astype(BF16),
        _rows(norm3_g),
        w_pe.astype(BF16), w_pg.astype(BF16),
        jnp.broadcast_to(final_g.astype(F32).reshape(1, 1, -1), (depth, 1, final_g.shape[0])),
    )
    for l in range(depth):
        x = _layer_call(l, l == depth - 1, x, p, mixer_consts, ffn_consts)
    return x
```

```python
import functools

import jax
import jax.numpy as jnp
from jax import lax
from jax.experimental import pallas as pl
from jax.experimental.pallas import tpu as pltpu

F32 = jnp.float32
BF16 = jnp.bfloat16

EPS = 1e-6
N_HEADS = 4
HEAD_DIM = 64
W_GRP = N_HEADS * HEAD_DIM
GMLP_CHUNK = 128
RGLRU_C = 8.0
POOL_WINDOWS = (2, 4, 8, 16)
POOL_HIST = 16
CONV_HIST = 8
LANES = 128
SUBLANES = 8
TS = 256
SCAN_SEG = TS // SUBLANES
SCAN_PITCH = SCAN_SEG + 1
FF_CHUNK = 256
LOG2E = 1.4426950408889634
VMEM_LIMIT = 56 * 1024 * 1024


def _dot(a, b):
    return jnp.dot(a, b, preferred_element_type=F32)


def _dot_nt(a, b):
    return lax.dot_general(a, b, (((1,), (1,)), ((), ())), preferred_element_type=F32)


def _dot_tn(a, b):
    return lax.dot_general(a, b, (((0,), (0,)), ((), ())), preferred_element_type=F32)


def _rms_norm(x, g):
    return x * lax.rsqrt(jnp.mean(x * x, axis=-1, keepdims=True) + EPS) * g


def _iota(shape, axis):
    return lax.broadcasted_iota(jnp.int32, shape, axis)


def _gelu_tanh(x):
    c = 0.7978845608028654
    inner = x * (c + (c * 0.044715) * (x * x))
    return x * (0.5 + 0.5 * jnp.tanh(inner))


def _split3(x):
    p1 = x.astype(BF16)
    r1 = x - p1.astype(F32)
    p2 = r1.astype(BF16)
    p3 = (r1 - p2.astype(F32)).astype(BF16)
    return p1, p2, p3


def _init_constants(ws_cat, wm_s, tri_s, avg_s, blk_s, head_s):
    ts = tri_s.shape[0]
    t_idx = _iota(ws_cat.shape, 0)
    s_idx = _iota(ws_cat.shape, 1) % GMLP_CHUNK
    wm_s[...] = jnp.where(s_idx <= t_idx, ws_cat, 0.0).astype(BF16)
    tt = _iota((ts, ts), 0)
    ss = _iota((ts, ts), 1)
    tri_s[...] = (ss <= tt).astype(BF16)
    same_head = (tt // HEAD_DIM) == (ss // HEAD_DIM)
    avg_s[...] = jnp.where(same_head, 1.0 / HEAD_DIM, 0.0).astype(BF16)
    head_s[...] = jnp.where(same_head, 1.0, 0.0)
    for n, (outer, _) in enumerate(_LEVELS[1:]):
        blk_s[n] = jnp.where((tt // outer) == (ss // outer), 1.0, 0.0)


def _gmlp(z_a, ln_g, ln_b, wm, bs_x):
    ts = z_a.shape[0]
    ab = _gelu_tanh(z_a)
    u, v = ab[:, :W_GRP], ab[:, W_GRP:]
    mu = jnp.mean(v, axis=-1, keepdims=True)
    var = jnp.mean(jnp.square(v - mu), axis=-1, keepdims=True)
    rstd = lax.rsqrt(var + EPS)
    yield
    vn = (v - mu) * rstd * ln_g + ln_b
    head = _iota((GMLP_CHUNK, W_GRP), 1) // HEAD_DIM
    outs = []
    for c in range(ts // GMLP_CHUNK):
        vc = vn[c * GMLP_CHUNK:(c + 1) * GMLP_CHUNK]
        rhs = jnp.concatenate([jnp.where(head == h, vc, 0.0) for h in range(N_HEADS)], axis=0).astype(BF16)
        sv = _dot(wm, rhs) + bs_x
        outs.append(u[c * GMLP_CHUNK:(c + 1) * GMLP_CHUNK] * sv)
    return jnp.concatenate(outs, axis=0)


def _causal_conv_slab(slab, x_tile, w, b):
    ts = x_tile.shape[0]
    k_width = w.shape[0]
    slab[CONV_HIST:CONV_HIST + ts, :] = x_tile
    y = b + w[k_width - 1:k_width] * x_tile
    for k in range(k_width - 1):
        shift = k_width - 1 - k
        y = y + w[k:k + 1] * slab[CONV_HIST - shift:CONV_HIST - shift + ts, :]
    slab[0:CONV_HIST, :] = slab[ts:ts + CONV_HIST, :]
    return y


def _rglru(z_b, conv_w, conv_b, wax, bax, lam, conv_slab, a_slab, b_slab, h_carry):
    ts = z_b.shape[0]
    n_tiles = W_GRP // LANES
    xb, gb = z_b[:, :W_GRP], z_b[:, W_GRP:]
    xc = jnp.concatenate(
        [_causal_conv_slab(conv_slab.at[c], xb[:, c * LANES:(c + 1) * LANES],
                           conv_w[:, c * LANES:(c + 1) * LANES], conv_b[:, c * LANES:(c + 1) * LANES])
         for c in range(n_tiles)], axis=1)
    ri = jax.nn.sigmoid(_dot(xc.astype(BF16), wax) + bax)
    r, i = ri[:, :W_GRP], ri[:, W_GRP:]
    log_a = (-RGLRU_C) * r * jax.nn.softplus(-lam)
    a = jnp.exp(log_a)
    th = jnp.tanh(log_a)
    mult = jnp.sqrt(-2.0 * th / (1.0 - th))
    bt = mult * (i * xc)
    assert ts == SUBLANES * SCAN_SEG
    hs = []
    for c in range(n_tiles):
        lanes = slice(c * LANES, (c + 1) * LANES)
        for sg in range(SUBLANES):
            rows = slice(sg * SCAN_SEG, (sg + 1) * SCAN_SEG)
            a_slab[c, sg * SCAN_PITCH:sg * SCAN_PITCH + SCAN_SEG, :] = a[rows, lanes]
            b_slab[c, sg * SCAN_PITCH:sg * SCAN_PITCH + SCAN_SEG, :] = bt[rows, lanes]
        a_cum = h_loc = None
        for j in range(SCAN_SEG):
            col = pl.ds(j, SUBLANES, stride=SCAN_PITCH)
            a_j, b_j = a_slab[c, col, :], b_slab[c, col, :]
            if j == 0:
                a_cum, h_loc = a_j, b_j
            else:
                h_loc = a_j * h_loc + b_j
                a_cum = a_j * a_cum
            a_slab[c, col, :] = a_cum
            b_slab[c, col, :] = h_loc
        carry = h_carry[0:1, lanes]
        segs = []
        for sg in range(SUBLANES):
            rows = slice(sg * SCAN_PITCH, sg * SCAN_PITCH + SCAN_SEG)
            segs.append(b_slab[c, rows, :] + a_slab[c, rows, :] * carry)
            carry = a_cum[sg:sg + 1] * carry + h_loc[sg:sg + 1]
        h_carry[:, lanes] = jnp.broadcast_to(carry, (h_carry.shape[0], LANES))
        hs.append(jnp.concatenate(segs, axis=0))
    h = jnp.concatenate(hs, axis=1)
    return h * _gelu_tanh(gb)


_LEVELS = ((256, 64), (64, 16), (16, 4), (4, 1))


def _level_slots(inner):
    slots = (0, 1, 2, 3) if inner == 1 else (1, 2, 3, None)
    return (slots[0:2], slots[2:4])


def _hgrn2_head_att(qd, kd, b_slab, blk_s):
    ts = qd.shape[0]
    lo1 = _iota((1, LANES), 1) < HEAD_DIM
    att = None
    for n_level, (outer, inner) in enumerate(_LEVELS[:3]):
        piece = max(inner, 2 * SUBLANES)
        lo = jnp.broadcast_to(lo1, (piece, LANES))
        pos = (_iota((piece, LANES), 0) % outer) // inner
        qcols, kcols = [], []
        for ja, jb in _level_slots(inner):
            jsel = jnp.where(lo, ja, -1 if jb is None else jb)
            qp, kp = [], []
            for base in range(0, ts, outer):
                ref = b_slab[base + ja * inner:base + ja * inner + 1, :]
                if jb is not None:
                    ref = jnp.where(lo1, ref, b_slab[base + jb * inner:base + jb * inner + 1, :])
                for r0 in range(base, base + outer, piece):
                    p0 = (r0 - base) // inner
                    p1 = (r0 - base + piece - 1) // inner
                    js = [j for j in (ja, jb) if j is not None]
                    rows = slice(r0, r0 + piece)
                    pos_r = pos + p0 if inner >= piece else pos
                    if any(p0 <= j <= p1 for j in js):
                        qp.append(jnp.where(pos_r == jsel, qd[rows] * jnp.exp2(b_slab[rows, :] - ref), 0.0))
                    else:
                        qp.append(jnp.zeros((piece, LANES), F32))
                    if any(p0 < j for j in js):
                        kp.append(jnp.where(pos_r < jsel, kd[rows] * jnp.exp2(ref - b_slab[rows, :]), 0.0))
                    else:
                        kp.append(jnp.zeros((piece, LANES), F32))
            qcols.append(jnp.concatenate(qp, axis=0))
            kcols.append(jnp.concatenate(kp, axis=0))
        a_l = _dot_nt(jnp.concatenate(qcols, axis=1).astype(BF16), jnp.concatenate(kcols, axis=1).astype(BF16))
        if outer < ts:
            a_l = a_l * blk_s[n_level - 1]
        att = a_l if att is None else att + a_l

    outer, inner = _LEVELS[3]
    assert inner == 1 and outer == 4
    lo = jnp.broadcast_to(lo1, (ts, LANES))
    m = _iota((ts, LANES), 0) % outer
    b0 = b_slab[0:ts, :]
    e = [None] + [jnp.exp2(b_slab[d:ts + d, :] - b0) for d in range(1, outer)]
    qcols, kcols = [], []
    for ja, jb in _level_slots(inner):
        jsel = jnp.where(lo, ja, jb)
        qcols.append(jnp.where(m == jsel, qd, 0.0))
        d = jsel - m
        f = jnp.where(d == 0, 1.0, 0.0)
        for dd in range(1, outer):
            f = jnp.where(d == dd, e[dd], f)
        kcols.append(kd * f)
    a_l = _dot_nt(jnp.concatenate(qcols, axis=1).astype(BF16), jnp.concatenate(kcols, axis=1).astype(BF16))
    return att + a_l * blk_s[len(_LEVELS) - 2]


def _dup_heads(x):
    out = []
    lo_half = _iota((x.shape[0], LANES), 1) < HEAD_DIM
    for pair in range(N_HEADS // 2):
        xp = x[:, pair * LANES:(pair + 1) * LANES]
        rot = pltpu.roll(xp, HEAD_DIM, 1)
        out.append(jnp.where(lo_half, xp, rot))
        out.append(jnp.where(lo_half, rot, xp))
    return out


def _hgrn2(z_c, lb, norm_g, state_t, b_slabs, tri_s, avg_s, blk_s, head_s):
    ts = z_c.shape[0]
    q, f, v, g = (z_c[:, n * W_GRP:(n + 1) * W_GRP] for n in range(4))
    qf = q * jax.nn.sigmoid(q)
    fgate = lb + (1.0 - lb) * jax.nn.sigmoid(f)
    log2f = jnp.log(fgate) * LOG2E
    kf = 1.0 - fgate
    tri = tri_s[...]
    p1, p2, p3 = _split3(log2f)
    bb = _dot(tri, p1) + _dot(tri, p2) + _dot(tri, p3)
    yield

    head = _iota((ts, W_GRP), 1) // HEAD_DIM
    v16 = v.astype(BF16)
    qh, kh = _dup_heads(qf), _dup_heads(kf)
    for h, bh in enumerate(_dup_heads(bb)):
        b_slabs[h, 0:ts, :] = bh
        b_slabs[h, ts:ts + SUBLANES, :] = jnp.zeros((SUBLANES, LANES), F32)
    yield
    o = _dot_nt((qf * jnp.exp2(bb)).astype(BF16), state_t[...].astype(BF16))
    for h in range(N_HEADS):
        att = _hgrn2_head_att(qh[h], kh[h], b_slabs.at[h], blk_s)
        o = o + _dot(att.astype(BF16), jnp.where(head == h, v16, jnp.zeros_like(v16)))
        yield

    bl = bb[ts - 1:ts, :]
    k_hat = (kf * jnp.exp2(bl - bb)).astype(BF16)
    upd = _dot_tn(v16, k_hat)
    state_t[...] = state_t[...] * jnp.exp2(bl) + upd * head_s[...]

    avg = avg_s[...]
    oo = o * o
    o_hi = oo.astype(BF16)
    o_lo = (oo - o_hi.astype(F32)).astype(BF16)
    ms = _dot(o_hi, avg) + _dot(o_lo, avg)
    o = o * lax.rsqrt(ms + EPS) * norm_g
    return o * (g * jax.nn.sigmoid(g))


def _pool(xd, dw, scale, pool_slab, tile_start):
    ts = xd.shape[0]
    lo_half = _iota((ts, LANES), 1) < HEAD_DIM
    pos = (tile_start + _iota((ts, LANES), 0) + 1).astype(F32)
    pooled = []
    for c in range(W_GRP // LANES):
        slab = pool_slab.at[c]
        x_tile = xd[:, c * LANES:(c + 1) * LANES]
        slab[POOL_HIST:POOL_HIST + ts, :] = x_tile
        w_lo, w_hi = POOL_WINDOWS[2 * c], POOL_WINDOWS[2 * c + 1]
        if w_lo % SUBLANES == 0 and w_hi == 2 * w_lo:
            acc = slab[POOL_HIST - w_lo:POOL_HIST + ts, :]
            for k in range(1, w_lo):
                acc = acc + slab[POOL_HIST - w_lo - k:POOL_HIST + ts - k, :]
            s_lo = acc[w_lo:]
            s_hi = s_lo + acc[:ts]
        else:
            acc = x_tile
            for k in range(1, w_lo):
                acc = acc + slab[POOL_HIST - k:POOL_HIST - k + ts, :]
            s_lo = acc
            for k in range(w_lo, w_hi):
                acc = acc + slab[POOL_HIST - k:POOL_HIST - k + ts, :]
            s_hi = acc
        slab[0:POOL_HIST, :] = slab[ts:ts + POOL_HIST, :]
        wsum = jnp.where(lo_half, s_lo, s_hi)
        wlen = jnp.where(lo_half, float(w_lo), float(w_hi))
        pooled.append(wsum / jnp.minimum(pos, wlen) - x_tile)
    return _dot(jnp.concatenate(pooled, axis=1).astype(BF16), dw) * scale


def _mixer_steps(layer, tile_start, x, consts, scratch, out):
    (g1_ref, win_ref, alng_ref, alnb_ref, aws_ref, absx_ref, bcw_ref, bcb_ref, bwax_ref, bbax_ref, blam_ref,
     clb_ref, cng_ref, dw_ref, dsc_ref, wout_ref) = consts
    conv_slab, a_slab, b_slab, h_carry, state_t, pool_slab, c_slabs, wm_s, tri_s, avg_s, blk_s, head_s = scratch
    h = _rms_norm(x, g1_ref[...]).astype(BF16)
    off_b, off_c, off_d = 2 * W_GRP, 4 * W_GRP, 8 * W_GRP

    y_a = yield from _gmlp(_dot(h, win_ref[:, 0:off_b]), alng_ref[...], alnb_ref[...], wm_s[...], absx_ref[...])
    yield
    y_b = _rglru(_dot(h, win_ref[:, off_b:off_c]), bcw_ref[...], bcb_ref[...], bwax_ref[...], bbax_ref[...],
                 blam_ref[...], conv_slab, a_slab, b_slab, h_carry)
    yield
    clb = clb_ref[...]
    e = jnp.exp(clb - jnp.max(clb, axis=0, keepdims=True))
    sm = e / jnp.sum(e, axis=0, keepdims=True)
    lb = jnp.zeros((1, W_GRP), F32)
    for j in range(1, layer + 1):
        lb = lb + sm[j:j + 1, :]
    y_c = yield from _hgrn2(_dot(h, win_ref[:, off_c:off_d]), lb, cng_ref[...], state_t, c_slabs, tri_s, avg_s,
                            blk_s, head_s)
    yield
    y_d = _pool(_dot(h, win_ref[:, off_d:off_d + W_GRP]), dw_ref[...], dsc_ref[...], pool_slab, tile_start)
    yield
    mix = jnp.concatenate([y_a, y_b, y_c, y_d], axis=1).astype(BF16)
    out.append(x + _dot(mix, wout_ref[...]))


def _ffn_steps(final, x, p_tile, consts, hbuf, out):
    g2_ref, wup_ref, cw_ref, cb_ref, wdown_ref, g3_ref, wpe_ref, wpg_ref, gf_ref = consts
    d_ff = wdown_ref.shape[0]
    n_chunks = d_ff // FF_CHUNK
    h2 = _rms_norm(x, g2_ref[...]).astype(BF16)

    def up(col0):
        return _dot(h2, wup_ref[:, col0:col0 + FF_CHUNK])

    def conv(col0, hf):
        outs = []
        for c in range(FF_CHUNK // LANES):
            lanes = slice(col0 + c * LANES, col0 + (c + 1) * LANES)
            outs.append(_causal_conv_slab(hbuf.at[(col0 // LANES) + c], hf[:, c * LANES:(c + 1) * LANES],
                                          cw_ref[:, lanes], cb_ref[:, lanes]))
        return jnp.concatenate(outs, axis=1)

    acc = x
    hg_next, hv_next = up(0), up(d_ff)
    yield
    for j in range(n_chunks):
        hg, hv = hg_next, hv_next
        if j + 1 < n_chunks:
            hg_next, hv_next = up((j + 1) * FF_CHUNK), up(d_ff + (j + 1) * FF_CHUNK)
        gt = conv(j * FF_CHUNK, hg)
        val = conv(d_ff + j * FF_CHUNK, hv)
        act = (_gelu_tanh(gt) * val).astype(BF16)
        acc = acc + _dot(act, wdown_ref[j * FF_CHUNK:(j + 1) * FF_CHUNK, :])
        yield
    x = acc
    rstd = lax.rsqrt(jnp.mean(x * x, axis=-1, keepdims=True) + EPS)
    pe = _dot(p_tile.astype(BF16), wpe_ref[...])
    yield
    gate = jax.nn.sigmoid(_dot((x * rstd * g3_ref[...]).astype(BF16), wpg_ref[...]))
    x = x + pe * gate
    if final:
        x = _rms_norm(x, gf_ref[...])
    out.append(x)


N_MIXER_CONSTS = 16
N_FFN_CONSTS = 9
PHASE_ORDER = "fm" * 9 + "ffmmfffm"


def _layer_kernel(layer, final, n_tiles, x_ref, p_ref, *refs):
    mixer_consts = refs[:N_MIXER_CONSTS]
    ffn_consts = refs[N_MIXER_CONSTS:N_MIXER_CONSTS + N_FFN_CONSTS]
    o_ref = refs[N_MIXER_CONSTS + N_FFN_CONSTS]
    (conv_slab, a_slab, b_slab, h_carry, state_t, pool_slab, c_slabs, wm_s, tri_s, avg_s, blk_s, head_s, hbuf,
     x1_buf) = refs[N_MIXER_CONSTS + N_FFN_CONSTS + 1:]
    i = pl.program_id(0)
    s = i % n_tiles

    @pl.when(i == 0)
    def _():
        x1_buf[...] = jnp.zeros_like(x1_buf)
        _init_constants(mixer_consts[4][...], wm_s, tri_s, avg_s, blk_s, head_s)

    @pl.when(s == 0)
    def _():
        conv_slab[:, 0:CONV_HIST, :] = jnp.zeros((conv_slab.shape[0], CONV_HIST, LANES), F32)
        pool_slab[:, 0:POOL_HIST, :] = jnp.zeros((pool_slab.shape[0], POOL_HIST, LANES), F32)
        h_carry[...] = jnp.zeros_like(h_carry)
        state_t[...] = jnp.zeros_like(state_t)

    @pl.when(jnp.maximum(i - 1, 0) % n_tiles == 0)
    def _():
        hbuf[:, 0:CONV_HIST, :] = jnp.zeros((hbuf.shape[0], CONV_HIST, LANES), F32)

    ts = x_ref.shape[0]
    mixer_out, ffn_out = [], []
    gens = [
        _ffn_steps(final, x1_buf[(i + 1) % 2], p_ref[...], ffn_consts, hbuf, ffn_out),
        _mixer_steps(layer, s * ts, x_ref[...], mixer_consts,
                     (conv_slab, a_slab, b_slab, h_carry, state_t, pool_slab, c_slabs, wm_s, tri_s, avg_s, blk_s,
                      head_s), mixer_out),
    ]
    ffn_gen, mixer_gen = gens
    phase = {"f": ffn_gen, "m": mixer_gen}
    for ch in PHASE_ORDER + "fm" * 4:
        try:
            next(phase[ch])
        except StopIteration:
            pass
    assert mixer_out and ffn_out
    x1_buf[i % 2] = mixer_out[0]
    o_ref[...] = ffn_out[0]


def _layer_spec(arr, layer):
    zeros = (0,) * (arr.ndim - 1)
    return pl.BlockSpec((None,) + arr.shape[1:], lambda i: (layer,) + zeros, pipeline_mode=pl.Buffered(1))


def _params():
    return pltpu.CompilerParams(dimension_semantics=("arbitrary",), vmem_limit_bytes=VMEM_LIMIT)


def _layer_call(layer, final, x, p, mixer_consts, ffn_consts):
    bsz, seq, d = x.shape
    n_tiles = seq // TS
    n_steps = bsz * n_tiles
    d_ff = ffn_consts[4].shape[1]
    consts = tuple(mixer_consts) + tuple(ffn_consts)
    assert len(mixer_consts) == N_MIXER_CONSTS and len(ffn_consts) == N_FFN_CONSTS

    def mixer_tile(i):
        t = jnp.minimum(i, n_steps - 1)
        return t // n_tiles, t % n_tiles

    def ffn_tile(i):
        t = jnp.maximum(i - 1, 0)
        return t // n_tiles, t % n_tiles

    return pl.pallas_call(
        functools.partial(_layer_kernel, layer, final, n_tiles),
        out_shape=jax.ShapeDtypeStruct(x.shape, x.dtype),
        grid=(n_steps + 1,),
        in_specs=[pl.BlockSpec((None, TS, d), lambda i: mixer_tile(i) + (0,)),
                  pl.BlockSpec((None, None, TS, p.shape[-1]), lambda i: (layer,) + ffn_tile(i) + (0,))]
        + [_layer_spec(c, layer) for c in consts],
        out_specs=pl.BlockSpec((None, TS, d), lambda i: ffn_tile(i) + (0,)),
        scratch_shapes=[
            pltpu.VMEM((W_GRP // LANES, CONV_HIST + TS, LANES), F32),
            pltpu.VMEM((W_GRP // LANES, SUBLANES * SCAN_PITCH, LANES), F32),
            pltpu.VMEM((W_GRP // LANES, SUBLANES * SCAN_PITCH, LANES), F32),
            pltpu.VMEM((SUBLANES, W_GRP), F32),
            pltpu.VMEM((W_GRP, W_GRP), F32),
            pltpu.VMEM((W_GRP // LANES, POOL_HIST + TS, LANES), F32),
            pltpu.VMEM((N_HEADS, TS + SUBLANES, LANES), F32),
            pltpu.VMEM((GMLP_CHUNK, N_HEADS * GMLP_CHUNK), BF16),
            pltpu.VMEM((TS, TS), BF16),
            pltpu.VMEM((W_GRP, W_GRP), BF16),
            pltpu.VMEM((len(_LEVELS) - 1, TS, TS), F32),
            pltpu.VMEM((W_GRP, W_GRP), F32),
            pltpu.VMEM((2 * d_ff // LANES, CONV_HIST + TS, LANES), F32),
            pltpu.VMEM((2, TS, d), F32),
        ],
        compiler_params=_params(),
        name=f"layer_l{layer}",
    )(x, p, *consts)


def _block_diag(w):
    depth, n_h, d_in, d_out = w.shape
    eye = jnp.eye(n_h, dtype=w.dtype)
    return jnp.einsum("lhde,hg->lhdge", w, eye).reshape(depth, n_h * d_in, n_h * d_out)


def _rows(v):
    return v.reshape(v.shape[0], 1, -1).astype(F32)


def kernel(x, p, norm1_g, w_in, a_ln_g, a_ln_b, a_ws, a_bs, b_conv_w, b_conv_b, b_wa, b_ba, b_wx, b_bx, b_lam, c_lb, c_norm_g, d_w, d_scale, w_out, norm2_g, w_up, ffn_conv_w, ffn_conv_b, w_down, norm3_g, w_pe, w_pg, final_g):
    depth = w_in.shape[0]
    d_ff = w_down.shape[1]
    assert x.shape[1] % TS == 0 and d_ff % FF_CHUNK == 0
    mixer_consts = (
        _rows(norm1_g),
        w_in.astype(BF16),
        _rows(a_ln_g), _rows(a_ln_b),
        a_ws.transpose(0, 2, 1, 3).reshape(depth, GMLP_CHUNK, N_HEADS * GMLP_CHUNK).astype(F32),
        jnp.repeat(a_bs.transpose(0, 2, 1), HEAD_DIM, axis=2).astype(F32),
        b_conv_w.astype(F32), _rows(b_conv_b),
        jnp.concatenate([_block_diag(b_wa), _block_diag(b_wx)], axis=2).astype(BF16),
        _rows(jnp.concatenate([b_ba, b_bx], axis=1)),
        _rows(b_lam),
        jnp.broadcast_to(c_lb.astype(F32), (depth,) + c_lb.shape),
        _rows(jnp.tile(c_norm_g, (1, N_HEADS))),
        _block_diag(d_w).astype(BF16), _rows(d_scale),
        w_out.astype(BF16),
    )
    ffn_consts = (
        _rows(norm2_g),
        w_up.astype(BF16),
        ffn_conv_w.astype(F32), _rows(ffn_conv_b),
        w_down.astype(BF16),
        _rows(norm3_g),
        w_pe.astype(BF16), w_pg.astype(BF16),
        jnp.broadcast_to(final_g.astype(F32).reshape(1, 1, -1), (depth, 1, final_g.shape[0])),
    )
    for l in range(depth):
        x = _layer_call(l, l == depth - 1, x, p, mixer_consts, ffn_consts)
    return x
```

```python
import functools

import jax
import jax.numpy as jnp
from jax import lax
from jax.experimental import pallas as pl
from jax.experimental.pallas import tpu as pltpu

F32 = jnp.float32
BF16 = jnp.bfloat16

EPS = 1e-6
N_HEADS = 4
HEAD_DIM = 64
W_GRP = N_HEADS * HEAD_DIM
GMLP_CHUNK = 128
RGLRU_C = 8.0
POOL_WINDOWS = (2, 4, 8, 16)
POOL_HIST = 16
CONV_HIST = 8
LANES = 128
SUBLANES = 8
TS = 256
SCAN_SEG = TS // SUBLANES
SCAN_PITCH = SCAN_SEG + 1
FF_CHUNK = 256
LOG2E = 1.4426950408889634
VMEM_LIMIT = 56 * 1024 * 1024


def _dot(a, b):
    return jnp.dot(a, b, preferred_element_type=F32)


def _dot_nt(a, b):
    return lax.dot_general(a, b, (((1,), (1,)), ((), ())), preferred_element_type=F32)


def _dot_tn(a, b):
    return lax.dot_general(a, b, (((0,), (0,)), ((), ())), preferred_element_type=F32)


def _rms_norm(x, g):
    return x * lax.rsqrt(jnp.mean(x * x, axis=-1, keepdims=True) + EPS) * g


def _iota(shape, axis):
    return lax.broadcasted_iota(jnp.int32, shape, axis)


def _gelu_tanh(x):
    c = 0.7978845608028654
    inner = x * (c + (c * 0.044715) * (x * x))
    return x * (0.5 + 0.5 * jnp.tanh(inner))


def _split3(x):
    p1 = x.astype(BF16)
    r1 = x - p1.astype(F32)
    p2 = r1.astype(BF16)
    p3 = (r1 - p2.astype(F32)).astype(BF16)
    return p1, p2, p3


def _init_constants(ws_cat, wm_s, tri_s, avg_s, blk_s, head_s):
    ts = tri_s.shape[0]
    t_idx = _iota(ws_cat.shape, 0)
    s_idx = _iota(ws_cat.shape, 1) % GMLP_CHUNK
    wm_s[...] = jnp.where(s_idx <= t_idx, ws_cat, 0.0).astype(BF16)
    tt = _iota((ts, ts), 0)
    ss = _iota((ts, ts), 1)
    tri_s[...] = (ss <= tt).astype(BF16)
    same_head = (tt // HEAD_DIM) == (ss // HEAD_DIM)
    avg_s[...] = jnp.where(same_head, 1.0 / HEAD_DIM, 0.0).astype(BF16)
    head_s[...] = jnp.where(same_head, 1.0, 0.0)
    for n, (outer, _) in enumerate(_LEVELS[1:]):
        blk_s[n] = jnp.where((tt // outer) == (ss // outer), 1.0, 0.0)


def _gmlp(z_a, ln_g, ln_b, wm, bs_x):
    ts = z_a.shape[0]
    ab = _gelu_tanh(z_a)
    u, v = ab[:, :W_GRP], ab[:, W_GRP:]
    mu = jnp.mean(v, axis=-1, keepdims=True)
    var = jnp.mean(jnp.square(v - mu), axis=-1, keepdims=True)
    rstd = lax.rsqrt(var + EPS)
    yield
    vn = (v - mu) * rstd * ln_g + ln_b
    head = _iota((GMLP_CHUNK, W_GRP), 1) // HEAD_DIM
    outs = []
    for c in range(ts // GMLP_CHUNK):
        vc = vn[c * GMLP_CHUNK:(c + 1) * GMLP_CHUNK]
        rhs = jnp.concatenate([jnp.where(head == h, vc, 0.0) for h in range(N_HEADS)], axis=0).astype(BF16)
        sv = _dot(wm, rhs) + bs_x
        outs.append(u[c * GMLP_CHUNK:(c + 1) * GMLP_CHUNK] * sv)
    return jnp.concatenate(outs, axis=0)


def _causal_conv_slab(slab, x_tile, w, b):
    ts = x_tile.shape[0]
    k_width = w.shape[0]
    slab[CONV_HIST:CONV_HIST + ts, :] = x_tile
    y = b + w[k_width - 1:k_width] * x_tile
    for k in range(k_width - 1):
        shift = k_width - 1 - k
        y = y + w[k:k + 1] * slab[CONV_HIST - shift:CONV_HIST - shift + ts, :]
    slab[0:CONV_HIST, :] = slab[ts:ts + CONV_HIST, :]
    return y


def _rglru(z_b, conv_w, conv_b, wax, bax, lam, conv_slab, a_slab, b_slab, h_carry):
    ts = z_b.shape[0]
    n_tiles = W_GRP // LANES
    xb, gb = z_b[:, :W_GRP], z_b[:, W_GRP:]
    xc = jnp.concatenate(
        [_causal_conv_slab(conv_slab.at[c], xb[:, c * LANES:(c + 1) * LANES],
                           conv_w[:, c * LANES:(c + 1) * LANES], conv_b[:, c * LANES:(c + 1) * LANES])
         for c in range(n_tiles)], axis=1)
    ri = jax.nn.sigmoid(_dot(xc.astype(BF16), wax) + bax)
    r, i = ri[:, :W_GRP], ri[:, W_GRP:]
    log_a = (-RGLRU_C) * r * jax.nn.softplus(-lam)
    a = jnp.exp(log_a)
    th = jnp.tanh(log_a)
    mult = jnp.sqrt(-2.0 * th / (1.0 - th))
    bt = mult * (i * xc)
    assert ts == SUBLANES * SCAN_SEG
    hs = []
    for c in range(n_tiles):
        lanes = slice(c * LANES, (c + 1) * LANES)
        for sg in range(SUBLANES):
            rows = slice(sg * SCAN_SEG, (sg + 1) * SCAN_SEG)
            a_slab[c, sg * SCAN_PITCH:sg * SCAN_PITCH + SCAN_SEG, :] = a[rows, lanes]
            b_slab[c, sg * SCAN_PITCH:sg * SCAN_PITCH + SCAN_SEG, :] = bt[rows, lanes]
        a_cum = h_loc = None
        for j in range(SCAN_SEG):
            col = pl.ds(j, SUBLANES, stride=SCAN_PITCH)
            a_j, b_j = a_slab[c, col, :], b_slab[c, col, :]
            if j == 0:
                a_cum, h_loc = a_j, b_j
            else:
                h_loc = a_j * h_loc + b_j
                a_cum = a_j * a_cum
            a_slab[c, col, :] = a_cum
            b_slab[c, col, :] = h_loc
        carry = h_carry[0:1, lanes]
        segs = []
        for sg in range(SUBLANES):
            rows = slice(sg * SCAN_PITCH, sg * SCAN_PITCH + SCAN_SEG)
            segs.append(b_slab[c, rows, :] + a_slab[c, rows, :] * carry)
            carry = a_cum[sg:sg + 1] * carry + h_loc[sg:sg + 1]
        h_carry[:, lanes] = jnp.broadcast_to(carry, (h_carry.shape[0], LANES))
        hs.append(jnp.concatenate(segs, axis=0))
    h = jnp.concatenate(hs, axis=1)
    return h * _gelu_tanh(gb)


_LEVELS = ((256, 64), (64, 16), (16, 4), (4, 1))


def _level_slots(inner):
    slots = (0, 1, 2, 3) if inner == 1 else (1, 2, 3, None)
    return (slots[0:2], slots[2:4])


def _hgrn2_head_att(qd, kd, b_slab, blk_s):
    ts = qd.shape[0]
    lo1 = _iota((1, LANES), 1) < HEAD_DIM
    att = None
    for n_level, (outer, inner) in enumerate(_LEVELS[:3]):
        piece = max(inner, 2 * SUBLANES)
        lo = jnp.broadcast_to(lo1, (piece, LANES))
        pos = (_iota((piece, LANES), 0) % outer) // inner
        qcols, kcols = [], []
        for ja, jb in _level_slots(inner):
            jsel = jnp.where(lo, ja, -1 if jb is None else jb)
            qp, kp = [], []
            for base in range(0, ts, outer):
                ref = b_slab[base + ja * inner:base + ja * inner + 1, :]
                if jb is not None:
                    ref = jnp.where(lo1, ref, b_slab[base + jb * inner:base + jb * inner + 1, :])
                for r0 in range(base, base + outer, piece):
                    p0 = (r0 - base) // inner
                    p1 = (r0 - base + piece - 1) // inner
                    js = [j for j in (ja, jb) if j is not None]
                    rows = slice(r0, r0 + piece)
                    pos_r = pos + p0 if inner >= piece else pos
                    if any(p0 <= j <= p1 for j in js):
                        qp.append(jnp.where(pos_r == jsel, qd[rows] * jnp.exp2(b_slab[rows, :] - ref), 0.0))
                    else:
                        qp.append(jnp.zeros((piece, LANES), F32))
                    if any(p0 < j for j in js):
                        kp.append(jnp.where(pos_r < jsel, kd[rows] * jnp.exp2(ref - b_slab[rows, :]), 0.0))
                    else:
                        kp.append(jnp.zeros((piece, LANES), F32))
            qcols.append(jnp.concatenate(qp, axis=0))
            kcols.append(jnp.concatenate(kp, axis=0))
        a_l = _dot_nt(jnp.concatenate(qcols, axis=1).astype(BF16), jnp.concatenate(kcols, axis=1).astype(BF16))
        if outer < ts:
            a_l = a_l * blk_s[n_level - 1]
        att = a_l if att is None else att + a_l

    outer, inner = _LEVELS[3]
    assert inner == 1 and outer == 4
    lo = jnp.broadcast_to(lo1, (ts, LANES))
    m = _iota((ts, LANES), 0) % outer
    b0 = b_slab[0:ts, :]
    e = [None] + [jnp.exp2(b_slab[d:ts + d, :] - b0) for d in range(1, outer)]
    qcols, kcols = [], []
    for ja, jb in _level_slots(inner):
        jsel = jnp.where(lo, ja, jb)
        qcols.append(jnp.where(m == jsel, qd, 0.0))
        d = jsel - m
        f = jnp.where(d == 0, 1.0, 0.0)
        for dd in range(1, outer):
            f = jnp.where(d == dd, e[dd], f)
        kcols.append(kd * f)
    a_l = _dot_nt(jnp.concatenate(qcols, axis=1).astype(BF16), jnp.concatenate(kcols, axis=1).astype(BF16))
    return att + a_l * blk_s[len(_LEVELS) - 2]


def _dup_heads(x):
    out = []
    lo_half = _iota((x.shape[0], LANES), 1) < HEAD_DIM
    for pair in range(N_HEADS // 2):
        xp = x[:, pair * LANES:(pair + 1) * LANES]
        rot = pltpu.roll(xp, HEAD_DIM, 1)
        out.append(jnp.where(lo_half, xp, rot))
        out.append(jnp.where(lo_half, rot, xp))
    return out


def _hgrn2(z_c, lb, norm_g, state_t, b_slabs, tri_s, avg_s, blk_s, head_s):
    ts = z_c.shape[0]
    q, f, v, g = (z_c[:, n * W_GRP:(n + 1) * W_GRP] for n in range(4))
    qf = q * jax.nn.sigmoid(q)
    fgate = lb + (1.0 - lb) * jax.nn.sigmoid(f)
    log2f = jnp.log(fgate) * LOG2E
    kf = 1.0 - fgate
    tri = tri_s[...]
    p1, p2, p3 = _split3(log2f)
    bb = _dot(tri, p1) + _dot(tri, p2) + _dot(tri, p3)
    yield

    head = _iota((ts, W_GRP), 1) // HEAD_DIM
    v16 = v.astype(BF16)
    qh, kh = _dup_heads(qf), _dup_heads(kf)
    for h, bh in enumerate(_dup_heads(bb)):
        b_slabs[h, 0:ts, :] = bh
        b_slabs[h, ts:ts + SUBLANES, :] = jnp.zeros((SUBLANES, LANES), F32)
    yield
    o = _dot_nt((qf * jnp.exp2(bb)).astype(BF16), state_t[...].astype(BF16))
    for h in range(N_HEADS):
        att = _hgrn2_head_att(qh[h], kh[h], b_slabs.at[h], blk_s)
        o = o + _dot(att.astype(BF16), jnp.where(head == h, v16, jnp.zeros_like(v16)))
        yield

    bl = bb[ts - 1:ts, :]
    k_hat = (kf * jnp.exp2(bl - bb)).astype(BF16)
    upd = _dot_tn(v16, k_hat)
    state_t[...] = state_t[...] * jnp.exp2(bl) + upd * head_s[...]

    avg = avg_s[...]
    oo = o * o
    o_hi = oo.astype(BF16)
    o_lo = (oo - o_hi.astype(F32)).astype(BF16)
    ms = _dot(o_hi, avg) + _dot(o_lo, avg)
    o = o * lax.rsqrt(ms + EPS) * norm_g
    return o * (g * jax.nn.sigmoid(g))


def _pool(xd, dw, scale, pool_slab, tile_start):
    ts = xd.shape[0]
    lo_half = _iota((ts, LANES), 1) < HEAD_DIM
    pos = (tile_start + _iota((ts, LANES), 0) + 1).astype(F32)
    pooled = []
    for c in range(W_GRP // LANES):
        slab = pool_slab.at[c]
        x_tile = xd[:, c * LANES:(c + 1) * LANES]
        slab[POOL_HIST:POOL_HIST + ts, :] = x_tile
        w_lo, w_hi = POOL_WINDOWS[2 * c], POOL_WINDOWS[2 * c + 1]
        if w_lo % SUBLANES == 0 and w_hi == 2 * w_lo:
            acc = slab[POOL_HIST - w_lo:POOL_HIST + ts, :]
            for k in range(1, w_lo):
                acc = acc + slab[POOL_HIST - w_lo - k:POOL_HIST + ts - k, :]
            s_lo = acc[w_lo:]
            s_hi = s_lo + acc[:ts]
        else:
            acc = x_tile
            for k in range(1, w_lo):
                acc = acc + slab[POOL_HIST - k:POOL_HIST - k + ts, :]
            s_lo = acc
            for k in range(w_lo, w_hi):
                acc = acc + slab[POOL_HIST - k:POOL_HIST - k + ts, :]
            s_hi = acc
        slab[0:POOL_HIST, :] = slab[ts:ts + POOL_HIST, :]
        wsum = jnp.where(lo_half, s_lo, s_hi)
        wlen = jnp.where(lo_half, float(w_lo), float(w_hi))
        pooled.append(wsum / jnp.minimum(pos, wlen) - x_tile)
    return _dot(jnp.concatenate(pooled, axis=1).astype(BF16), dw) * scale


def _mixer_steps(layer, tile_start, x, consts, scratch, out):
    (g1_ref, win_ref, alng_ref, alnb_ref, aws_ref, absx_ref, bcw_ref, bcb_ref, bwax_ref, bbax_ref, blam_ref,
     clb_ref, cng_ref, dw_ref, dsc_ref, wout_ref) = consts
    conv_slab, a_slab, b_slab, h_carry, state_t, pool_slab, c_slabs, wm_s, tri_s, avg_s, blk_s, head_s = scratch
    h = _rms_norm(x, g1_ref[...]).astype(BF16)
    off_b, off_c, off_d = 2 * W_GRP, 4 * W_GRP, 8 * W_GRP

    y_a = yield from _gmlp(_dot(h, win_ref[:, 0:off_b]), alng_ref[...], alnb_ref[...], wm_s[...], absx_ref[...])
    yield
    y_b = _rglru(_dot(h, win_ref[:, off_b:off_c]), bcw_ref[...], bcb_ref[...], bwax_ref[...], bbax_ref[...],
                 blam_ref[...], conv_slab, a_slab, b_slab, h_carry)
    yield
    clb = clb_ref[...]
    e = jnp.exp(clb - jnp.max(clb, axis=0, keepdims=True))
    sm = e / jnp.sum(e, axis=0, keepdims=True)
    lb = jnp.zeros((1, W_GRP), F32)
    for j in range(1, layer + 1):
        lb = lb + sm[j:j + 1, :]
    y_c = yield from _hgrn2(_dot(h, win_ref[:, off_c:off_d]), lb, cng_ref[...], state_t, c_slabs, tri_s, avg_s,
                            blk_s, head_s)
    yield
    y_d = _pool(_dot(h, win_ref[:, off_d:off_d + W_GRP]), dw_ref[...], dsc_ref[...], pool_slab, tile_start)
    yield
    mix = jnp.concatenate([y_a, y_b, y_c, y_d], axis=1).astype(BF16)
    out.append(x + _dot(mix, wout_ref[...]))


def _ffn_steps(final, x, p_tile, consts, hbuf, out):
    g2_ref, wup_ref, cw_ref, cb_ref, wdown_ref, g3_ref, wpe_ref, wpg_ref, gf_ref = consts
    d_ff = wdown_ref.shape[0]
    n_chunks = d_ff // FF_CHUNK
    h2 = _rms_norm(x, g2_ref[...]).astype(BF16)

    def up(col0):
        return _dot(h2, wup_ref[:, col0:col0 + FF_CHUNK])

    def conv(col0, hf):
        outs = []
        for c in range(FF_CHUNK // LANES):
            lanes = slice(col0 + c * LANES, col0 + (c + 1) * LANES)
            outs.append(_causal_conv_slab(hbuf.at[(col0 // LANES) + c], hf[:, c * LANES:(c + 1) * LANES],
                                          cw_ref[:, lanes], cb_ref[:, lanes]))
        return jnp.concatenate(outs, axis=1)

    acc = x
    hg_next, hv_next = up(0), up(d_ff)
    yield
    for j in range(n_chunks):
        hg, hv = hg_next, hv_next
        if j + 1 < n_chunks:
            hg_next, hv_next = up((j + 1) * FF_CHUNK), up(d_ff + (j + 1) * FF_CHUNK)
        gt = conv(j * FF_CHUNK, hg)
        val = conv(d_ff + j * FF_CHUNK, hv)
        act = (_gelu_tanh(gt) * val).astype(BF16)
        acc = acc + _dot(act, wdown_ref[j * FF_CHUNK:(j + 1) * FF_CHUNK, :])
        yield
    x = acc
    rstd = lax.rsqrt(jnp.mean(x * x, axis=-1, keepdims=True) + EPS)
    pe = _dot(p_tile.astype(BF16), wpe_ref[...])
    yield
    gate = jax.nn.sigmoid(_dot((x * rstd * g3_ref[...]).astype(BF16), wpg_ref[...]))
    x = x + pe * gate
    if final:
        x = _rms_norm(x, gf_ref[...])
    out.append(x)


N_MIXER_CONSTS = 16
N_FFN_CONSTS = 9
PHASE_ORDER = "fm" * 9 + "ffmmfffm"
BIG_WEIGHTS = (1, 15, N_MIXER_CONSTS + 1, N_MIXER_CONSTS + 4, N_MIXER_CONSTS + 6, N_MIXER_CONSTS + 7)
CAST_ROWS = 64


def _cast_weights(layer, srcs, dsts, stages, sems):
    for src, dst in zip(srcs, dsts):
        k, n = dst.shape
        stage = stages[n]
        n_chunks = k // CAST_ROWS

        def chunk_copy(c, slot, src=src, stage=stage):
            rows = pl.ds(pl.multiple_of(c * CAST_ROWS, CAST_ROWS), CAST_ROWS)
            return pltpu.make_async_copy(src.at[layer, rows, :], stage.at[slot], sems.at[slot])

        chunk_copy(0, 0).start()

        def body(c, carry, dst=dst, stage=stage, chunk_copy=chunk_copy, n_chunks=n_chunks):
            slot = c % 2

            @pl.when(c + 1 < n_chunks)
            def _():
                chunk_copy(c + 1, 1 - slot).start()

            chunk_copy(c, slot).wait()
            rows = pl.ds(pl.multiple_of(c * CAST_ROWS, CAST_ROWS), CAST_ROWS)
            dst[rows, :] = stage[slot].astype(BF16)
            return carry

        lax.fori_loop(0, n_chunks, body, 0)


def _layer_kernel(layer, final, n_tiles, x_ref, p_ref, *refs):
    n_consts = N_MIXER_CONSTS + N_FFN_CONSTS
    consts = list(refs[:n_consts])
    o_ref = refs[n_consts]
    (conv_slab, a_slab, b_slab, h_carry, state_t, pool_slab, c_slabs, wm_s, tri_s, avg_s, blk_s, head_s, hbuf,
     x1_buf) = refs[n_consts + 1:n_consts + 15]
    w16 = refs[n_consts + 15:n_consts + 15 + len(BIG_WEIGHTS)]
    stage_refs = refs[n_consts + 15 + len(BIG_WEIGHTS):-1]
    cast_sems = refs[-1]
    w_hbm = [consts[k] for k in BIG_WEIGHTS]
    for k, w in zip(BIG_WEIGHTS, w16):
        consts[k] = w
    mixer_consts = consts[:N_MIXER_CONSTS]
    ffn_consts = consts[N_MIXER_CONSTS:]
    i = pl.program_id(0)
    s = i % n_tiles

    @pl.when(i == 0)
    def _():
        x1_buf[...] = jnp.zeros_like(x1_buf)
        _cast_weights(layer, w_hbm, w16, {s.shape[2]: s for s in stage_refs}, cast_sems)
        _init_constants(mixer_consts[4][...], wm_s, tri_s, avg_s, blk_s, head_s)

    @pl.when(s == 0)
    def _():
        conv_slab[:, 0:CONV_HIST, :] = jnp.zeros((conv_slab.shape[0], CONV_HIST, LANES), F32)
        pool_slab[:, 0:POOL_HIST, :] = jnp.zeros((pool_slab.shape[0], POOL_HIST, LANES), F32)
        h_carry[...] = jnp.zeros_like(h_carry)
        state_t[...] = jnp.zeros_like(state_t)

    @pl.when(jnp.maximum(i - 1, 0) % n_tiles == 0)
    def _():
        hbuf[:, 0:CONV_HIST, :] = jnp.zeros((hbuf.shape[0], CONV_HIST, LANES), F32)

    ts = x_ref.shape[0]
    mixer_out, ffn_out = [], []
    gens = [
        _ffn_steps(final, x1_buf[(i + 1) % 2], p_ref[...], ffn_consts, hbuf, ffn_out),
        _mixer_steps(layer, s * ts, x_ref[...], mixer_consts,
                     (conv_slab, a_slab, b_slab, h_carry, state_t, pool_slab, c_slabs, wm_s, tri_s, avg_s, blk_s,
                      head_s), mixer_out),
    ]
    ffn_gen, mixer_gen = gens
    phase = {"f": ffn_gen, "m": mixer_gen}
    for ch in PHASE_ORDER + "fm" * 4:
        try:
            next(phase[ch])
        except StopIteration:
            pass
    assert mixer_out and ffn_out
    x1_buf[i % 2] = mixer_out[0]
    o_ref[...] = ffn_out[0]


def _layer_spec(arr, layer):
    zeros = (0,) * (arr.ndim - 1)
    return pl.BlockSpec((None,) + arr.shape[1:], lambda i: (layer,) + zeros, pipeline_mode=pl.Buffered(1))


def _params():
    return pltpu.CompilerParams(dimension_semantics=("arbitrary",), vmem_limit_bytes=VMEM_LIMIT)


def _layer_call(layer, final, x, p, mixer_consts, ffn_consts):
    bsz, seq, d = x.shape
    n_tiles = seq // TS
    n_steps = bsz * n_tiles
    d_ff = ffn_consts[4].shape[1]
    consts = tuple(mixer_consts) + tuple(ffn_consts)
    assert len(mixer_consts) == N_MIXER_CONSTS and len(ffn_consts) == N_FFN_CONSTS

    def mixer_tile(i):
        t = jnp.minimum(i, n_steps - 1)
        return t // n_tiles, t % n_tiles

    def ffn_tile(i):
        t = jnp.maximum(i - 1, 0)
        return t // n_tiles, t % n_tiles

    return pl.pallas_call(
        functools.partial(_layer_kernel, layer, final, n_tiles),
        out_shape=jax.ShapeDtypeStruct(x.shape, x.dtype),
        grid=(n_steps + 1,),
        in_specs=[pl.BlockSpec((None, TS, d), lambda i: mixer_tile(i) + (0,)),
                  pl.BlockSpec((None, None, TS, p.shape[-1]), lambda i: (layer,) + ffn_tile(i) + (0,))]
        + [pl.BlockSpec(memory_space=pl.ANY) if k in BIG_WEIGHTS else _layer_spec(c, layer)
           for k, c in enumerate(consts)],
        out_specs=pl.BlockSpec((None, TS, d), lambda i: ffn_tile(i) + (0,)),
        scratch_shapes=[
            pltpu.VMEM((W_GRP // LANES, CONV_HIST + TS, LANES), F32),
            pltpu.VMEM((W_GRP // LANES, SUBLANES * SCAN_PITCH, LANES), F32),
            pltpu.VMEM((W_GRP // LANES, SUBLANES * SCAN_PITCH, LANES), F32),
            pltpu.VMEM((SUBLANES, W_GRP), F32),
            pltpu.VMEM((W_GRP, W_GRP), F32),
            pltpu.VMEM((W_GRP // LANES, POOL_HIST + TS, LANES), F32),
            pltpu.VMEM((N_HEADS, TS + SUBLANES, LANES), F32),
            pltpu.VMEM((GMLP_CHUNK, N_HEADS * GMLP_CHUNK), BF16),
            pltpu.VMEM((TS, TS), BF16),
            pltpu.VMEM((W_GRP, W_GRP), BF16),
            pltpu.VMEM((len(_LEVELS) - 1, TS, TS), F32),
            pltpu.VMEM((W_GRP, W_GRP), F32),
            pltpu.VMEM((2 * d_ff // LANES, CONV_HIST + TS, LANES), F32),
            pltpu.VMEM((2, TS, d), F32),
        ]
        + [pltpu.VMEM(consts[k].shape[1:], BF16) for k in BIG_WEIGHTS]
        + [pltpu.VMEM((2, CAST_ROWS, n), F32) for n in sorted({consts[k].shape[2] for k in BIG_WEIGHTS})]
        + [pltpu.SemaphoreType.DMA((2,))],
        compiler_params=_params(),
        name=f"layer_l{layer}",
    )(x, p, *consts)


def _block_diag(w):
    depth, n_h, d_in, d_out = w.shape
    eye = jnp.eye(n_h, dtype=w.dtype)
    return jnp.einsum("lhde,hg->lhdge", w, eye).reshape(depth, n_h * d_in, n_h * d_out)


def _rows(v):
    return v.reshape(v.shape[0], 1, -1).astype(F32)


def kernel(x, p, norm1_g, w_in, a_ln_g, a_ln_b, a_ws, a_bs, b_conv_w, b_conv_b, b_wa, b_ba, b_wx, b_bx, b_lam, c_lb, c_norm_g, d_w, d_scale, w_out, norm2_g, w_up, ffn_conv_w, ffn_conv_b, w_down, norm3_g, w_pe, w_pg, final_g):
    depth = w_in.shape[0]
    d_ff = w_down.shape[1]
    assert x.shape[1] % TS == 0 and d_ff % FF_CHUNK == 0
    mixer_consts = (
        _rows(norm1_g),
        w_in.astype(F32),
        _rows(a_ln_g), _rows(a_ln_b),
        a_ws.transpose(0, 2, 1, 3).reshape(depth, GMLP_CHUNK, N_HEADS * GMLP_CHUNK).astype(F32),
        jnp.repeat(a_bs.transpose(0, 2, 1), HEAD_DIM, axis=2).astype(F32),
        b_conv_w.astype(F32), _rows(b_conv_b),
        jnp.concatenate([_block_diag(b_wa), _block_diag(b_wx)], axis=2).astype(BF16),
        _rows(jnp.concatenate([b_ba, b_bx], axis=1)),
        _rows(b_lam),
        jnp.broadcast_to(c_lb.astype(F32), (depth,) + c_lb.shape),
        _rows(jnp.tile(c_norm_g, (1, N_HEADS))),
        _block_diag(d_w).astype(BF16), _rows(d_scale),
        w_out.astype(F32),
    )
    ffn_consts = (
        _rows(norm2_g),
        w_up.astype(F32),
        ffn_conv_w.astype(F32), _rows(ffn_conv_b),
        w_down.astype(F32),
        _rows(norm3_g),
        w_pe.astype(F32), w_pg.astype(F32),
        jnp.broadcast_to(final_g.astype(F32).reshape(1, 1, -1), (depth, 1, final_g.shape[0])),
    )
    for l in range(depth):
        x = _layer_call(l, l == depth - 1, x, p, mixer_consts, ffn_consts)
    return x
```

```python
import functools

import jax
import jax.numpy as jnp
from jax import lax
from jax.experimental import pallas as pl
from jax.experimental.pallas import tpu as pltpu

F32 = jnp.float32
BF16 = jnp.bfloat16

EPS = 1e-6
N_HEADS = 4
HEAD_DIM = 64
W_GRP = N_HEADS * HEAD_DIM
GMLP_CHUNK = 128
RGLRU_C = 8.0
POOL_WINDOWS = (2, 4, 8, 16)
POOL_HIST = 16
CONV_HIST = 8
LANES = 128
SUBLANES = 8
TS = 256
SCAN_SEG = TS // SUBLANES
SCAN_PITCH = SCAN_SEG + 1
FF_CHUNK = 256
LOG2E = 1.4426950408889634
VMEM_LIMIT = 56 * 1024 * 1024


def _dot(a, b):
    return jnp.dot(a, b, preferred_element_type=F32)


def _dot_nt(a, b):
    return lax.dot_general(a, b, (((1,), (1,)), ((), ())), preferred_element_type=F32)


def _dot_tn(a, b):
    return lax.dot_general(a, b, (((0,), (0,)), ((), ())), preferred_element_type=F32)


def _rms_norm(x, g):
    return x * lax.rsqrt(jnp.mean(x * x, axis=-1, keepdims=True) + EPS) * g


def _iota(shape, axis):
    return lax.broadcasted_iota(jnp.int32, shape, axis)


def _gelu_tanh(x):
    c = 0.7978845608028654
    inner = x * (c + (c * 0.044715) * (x * x))
    return x * (0.5 + 0.5 * jnp.tanh(inner))


def _split3(x):
    p1 = x.astype(BF16)
    r1 = x - p1.astype(F32)
    p2 = r1.astype(BF16)
    p3 = (r1 - p2.astype(F32)).astype(BF16)
    return p1, p2, p3


def _init_constants(ws_cat, wm_s, tri_s, avg_s, blk_s, head_s):
    ts = tri_s.shape[0]
    t_idx = _iota(ws_cat.shape, 0)
    s_idx = _iota(ws_cat.shape, 1) % GMLP_CHUNK
    wm_s[...] = jnp.where(s_idx <= t_idx, ws_cat, 0.0).astype(BF16)
    tt = _iota((ts, ts), 0)
    ss = _iota((ts, ts), 1)
    tri_s[...] = (ss <= tt).astype(BF16)
    same_head = (tt // HEAD_DIM) == (ss // HEAD_DIM)
    avg_s[...] = jnp.where(same_head, 1.0 / HEAD_DIM, 0.0).astype(BF16)
    head_s[...] = jnp.where(same_head, 1.0, 0.0)
    for n, (outer, _) in enumerate(_LEVELS[1:]):
        blk_s[n] = jnp.where((tt // outer) == (ss // outer), 1.0, 0.0)


def _gmlp(z_a, ln_g, ln_b, wm, bs_x):
    ts = z_a.shape[0]
    ab = _gelu_tanh(z_a)
    u, v = ab[:, :W_GRP], ab[:, W_GRP:]
    mu = jnp.mean(v, axis=-1, keepdims=True)
    var = jnp.mean(jnp.square(v - mu), axis=-1, keepdims=True)
    rstd = lax.rsqrt(var + EPS)
    yield
    vn = (v - mu) * rstd * ln_g + ln_b
    head = _iota((GMLP_CHUNK, W_GRP), 1) // HEAD_DIM
    outs = []
    for c in range(ts // GMLP_CHUNK):
        vc = vn[c * GMLP_CHUNK:(c + 1) * GMLP_CHUNK]
        rhs = jnp.concatenate([jnp.where(head == h, vc, 0.0) for h in range(N_HEADS)], axis=0).astype(BF16)
        sv = _dot(wm, rhs) + bs_x
        outs.append(u[c * GMLP_CHUNK:(c + 1) * GMLP_CHUNK] * sv)
    return jnp.concatenate(outs, axis=0)


def _causal_conv_slab(slab, x_tile, w, b):
    ts = x_tile.shape[0]
    k_width = w.shape[0]
    slab[CONV_HIST:CONV_HIST + ts, :] = x_tile
    y = b + w[k_width - 1:k_width] * x_tile
    for k in range(k_width - 1):
        shift = k_width - 1 - k
        y = y + w[k:k + 1] * slab[CONV_HIST - shift:CONV_HIST - shift + ts, :]
    slab[0:CONV_HIST, :] = slab[ts:ts + CONV_HIST, :]
    return y


def _rglru(z_b, conv_w, conv_b, wax, bax, lam, conv_slab, a_slab, b_slab, h_carry):
    ts = z_b.shape[0]
    n_tiles = W_GRP // LANES
    xb, gb = z_b[:, :W_GRP], z_b[:, W_GRP:]
    xc = jnp.concatenate(
        [_causal_conv_slab(conv_slab.at[c], xb[:, c * LANES:(c + 1) * LANES],
                           conv_w[:, c * LANES:(c + 1) * LANES], conv_b[:, c * LANES:(c + 1) * LANES])
         for c in range(n_tiles)], axis=1)
    ri = jax.nn.sigmoid(_dot(xc.astype(BF16), wax) + bax)
    r, i = ri[:, :W_GRP], ri[:, W_GRP:]
    log_a = (-RGLRU_C) * r * jax.nn.softplus(-lam)
    a = jnp.exp(log_a)
    th = jnp.tanh(log_a)
    mult = jnp.sqrt(-2.0 * th / (1.0 - th))
    bt = mult * (i * xc)
    assert ts == SUBLANES * SCAN_SEG
    hs = []
    for c in range(n_tiles):
        lanes = slice(c * LANES, (c + 1) * LANES)
        for sg in range(SUBLANES):
            rows = slice(sg * SCAN_SEG, (sg + 1) * SCAN_SEG)
            a_slab[c, sg * SCAN_PITCH:sg * SCAN_PITCH + SCAN_SEG, :] = a[rows, lanes]
            b_slab[c, sg * SCAN_PITCH:sg * SCAN_PITCH + SCAN_SEG, :] = bt[rows, lanes]
        a_cum = h_loc = None
        for j in range(SCAN_SEG):
            col = pl.ds(j, SUBLANES, stride=SCAN_PITCH)
            a_j, b_j = a_slab[c, col, :], b_slab[c, col, :]
            if j == 0:
                a_cum, h_loc = a_j, b_j
            else:
                h_loc = a_j * h_loc + b_j
                a_cum = a_j * a_cum
            a_slab[c, col, :] = a_cum
            b_slab[c, col, :] = h_loc
        carry = h_carry[0:1, lanes]
        segs = []
        for sg in range(SUBLANES):
            rows = slice(sg * SCAN_PITCH, sg * SCAN_PITCH + SCAN_SEG)
            segs.append(b_slab[c, rows, :] + a_slab[c, rows, :] * carry)
            carry = a_cum[sg:sg + 1] * carry + h_loc[sg:sg + 1]
        h_carry[:, lanes] = jnp.broadcast_to(carry, (h_carry.shape[0], LANES))
        hs.append(jnp.concatenate(segs, axis=0))
    h = jnp.concatenate(hs, axis=1)
    return h * _gelu_tanh(gb)


_LEVELS = ((256, 64), (64, 16), (16, 4), (4, 1))


def _level_slots(inner):
    slots = (0, 1, 2, 3) if inner == 1 else (1, 2, 3, None)
    return (slots[0:2], slots[2:4])


def _hgrn2_head_att(qd, kd, b_slab, blk_s):
    ts = qd.shape[0]
    lo1 = _iota((1, LANES), 1) < HEAD_DIM
    att = None
    for n_level, (outer, inner) in enumerate(_LEVELS[:3]):
        piece = max(inner, 2 * SUBLANES)
        lo = jnp.broadcast_to(lo1, (piece, LANES))
        pos = (_iota((piece, LANES), 0) % outer) // inner
        qcols, kcols = [], []
        for ja, jb in _level_slots(inner):
            jsel = jnp.where(lo, ja, -1 if jb is None else jb)
            qp, kp = [], []
            for base in range(0, ts, outer):
                ref = b_slab[base + ja * inner:base + ja * inner + 1, :]
                if jb is not None:
                    ref = jnp.where(lo1, ref, b_slab[base + jb * inner:base + jb * inner + 1, :])
                for r0 in range(base, base + outer, piece):
                    p0 = (r0 - base) // inner
                    p1 = (r0 - base + piece - 1) // inner
                    js = [j for j in (ja, jb) if j is not None]
                    rows = slice(r0, r0 + piece)
                    pos_r = pos + p0 if inner >= piece else pos
                    if any(p0 <= j <= p1 for j in js):
                        qp.append(jnp.where(pos_r == jsel, qd[rows] * jnp.exp2(b_slab[rows, :] - ref), 0.0))
                    else:
                        qp.append(jnp.zeros((piece, LANES), F32))
                    if any(p0 < j for j in js):
                        kp.append(jnp.where(pos_r < jsel, kd[rows] * jnp.exp2(ref - b_slab[rows, :]), 0.0))
                    else:
                        kp.append(jnp.zeros((piece, LANES), F32))
            qcols.append(jnp.concatenate(qp, axis=0))
            kcols.append(jnp.concatenate(kp, axis=0))
        a_l = _dot_nt(jnp.concatenate(qcols, axis=1).astype(BF16), jnp.concatenate(kcols, axis=1).astype(BF16))
        if outer < ts:
            a_l = a_l * blk_s[n_level - 1]
        att = a_l if att is None else att + a_l

    outer, inner = _LEVELS[3]
    assert inner == 1 and outer == 4
    lo = jnp.broadcast_to(lo1, (ts, LANES))
    m = _iota((ts, LANES), 0) % outer
    b0 = b_slab[0:ts, :]
    e = [None] + [jnp.exp2(b_slab[d:ts + d, :] - b0) for d in range(1, outer)]
    qcols, kcols = [], []
    for ja, jb in _level_slots(inner):
        jsel = jnp.where(lo, ja, jb)
        qcols.append(jnp.where(m == jsel, qd, 0.0))
        d = jsel - m
        f = jnp.where(d == 0, 1.0, 0.0)
        for dd in range(1, outer):
            f = jnp.where(d == dd, e[dd], f)
        kcols.append(kd * f)
    a_l = _dot_nt(jnp.concatenate(qcols, axis=1).astype(BF16), jnp.concatenate(kcols, axis=1).astype(BF16))
    return att + a_l * blk_s[len(_LEVELS) - 2]


def _dup_heads(x):
    out = []
    lo_half = _iota((x.shape[0], LANES), 1) < HEAD_DIM
    for pair in range(N_HEADS // 2):
        xp = x[:, pair * LANES:(pair + 1) * LANES]
        rot = pltpu.roll(xp, HEAD_DIM, 1)
        out.append(jnp.where(lo_half, xp, rot))
        out.append(jnp.where(lo_half, rot, xp))
    return out


def _hgrn2(z_c, lb, norm_g, state_t, b_slabs, tri_s, avg_s, blk_s, head_s):
    ts = z_c.shape[0]
    q, f, v, g = (z_c[:, n * W_GRP:(n + 1) * W_GRP] for n in range(4))
    qf = q * jax.nn.sigmoid(q)
    fgate = lb + (1.0 - lb) * jax.nn.sigmoid(f)
    log2f = jnp.log(fgate) * LOG2E
    kf = 1.0 - fgate
    tri = tri_s[...]
    p1, p2, p3 = _split3(log2f)
    bb = _dot(tri, p1) + _dot(tri, p2) + _dot(tri, p3)
    yield

    head = _iota((ts, W_GRP), 1) // HEAD_DIM
    v16 = v.astype(BF16)
    qh, kh = _dup_heads(qf), _dup_heads(kf)
    for h, bh in enumerate(_dup_heads(bb)):
        b_slabs[h, 0:ts, :] = bh
        b_slabs[h, ts:ts + SUBLANES, :] = jnp.zeros((SUBLANES, LANES), F32)
    yield
    o = _dot_nt((qf * jnp.exp2(bb)).astype(BF16), state_t[...].astype(BF16))
    for h in range(N_HEADS):
        att = _hgrn2_head_att(qh[h], kh[h], b_slabs.at[h], blk_s)
        o = o + _dot(att.astype(BF16), jnp.where(head == h, v16, jnp.zeros_like(v16)))
        yield

    bl = bb[ts - 1:ts, :]
    k_hat = (kf * jnp.exp2(bl - bb)).astype(BF16)
    upd = _dot_tn(v16, k_hat)
    state_t[...] = state_t[...] * jnp.exp2(bl) + upd * head_s[...]

    avg = avg_s[...]
    oo = o * o
    o_hi = oo.astype(BF16)
    o_lo = (oo - o_hi.astype(F32)).astype(BF16)
    ms = _dot(o_hi, avg) + _dot(o_lo, avg)
    o = o * lax.rsqrt(ms + EPS) * norm_g
    return o * (g * jax.nn.sigmoid(g))


def _pool(xd, dw, scale, pool_slab, tile_start):
    ts = xd.shape[0]
    lo_half = _iota((ts, LANES), 1) < HEAD_DIM
    pos = (tile_start + _iota((ts, LANES), 0) + 1).astype(F32)
    pooled = []
    for c in range(W_GRP // LANES):
        slab = pool_slab.at[c]
        x_tile = xd[:, c * LANES:(c + 1) * LANES]
        slab[POOL_HIST:POOL_HIST + ts, :] = x_tile
        w_lo, w_hi = POOL_WINDOWS[2 * c], POOL_WINDOWS[2 * c + 1]
        if w_lo % SUBLANES == 0 and w_hi == 2 * w_lo:
            acc = slab[POOL_HIST - w_lo:POOL_HIST + ts, :]
            for k in range(1, w_lo):
                acc = acc + slab[POOL_HIST - w_lo - k:POOL_HIST + ts - k, :]
            s_lo = acc[w_lo:]
            s_hi = s_lo + acc[:ts]
        else:
            acc = x_tile
            for k in range(1, w_lo):
                acc = acc + slab[POOL_HIST - k:POOL_HIST - k + ts, :]
            s_lo = acc
            for k in range(w_lo, w_hi):
                acc = acc + slab[POOL_HIST - k:POOL_HIST - k + ts, :]
            s_hi = acc
        slab[0:POOL_HIST, :] = slab[ts:ts + POOL_HIST, :]
        wsum = jnp.where(lo_half, s_lo, s_hi)
        wlen = jnp.where(lo_half, float(w_lo), float(w_hi))
        pooled.append(wsum / jnp.minimum(pos, wlen) - x_tile)
    return _dot(jnp.concatenate(pooled, axis=1).astype(BF16), dw) * scale


def _mixer_steps(layer, tile_start, x, consts, scratch, out):
    (g1_ref, win_ref, alng_ref, alnb_ref, aws_ref, absx_ref, bcw_ref, bcb_ref, bwax_ref, bbax_ref, blam_ref,
     clb_ref, cng_ref, dw_ref, dsc_ref, wout_ref) = consts
    conv_slab, a_slab, b_slab, h_carry, state_t, pool_slab, c_slabs, wm_s, tri_s, avg_s, blk_s, head_s = scratch
    h = _rms_norm(x, g1_ref[...]).astype(BF16)
    off_b, off_c, off_d = 2 * W_GRP, 4 * W_GRP, 8 * W_GRP

    y_b = _rglru(_dot(h, win_ref[:, off_b:off_c]), bcw_ref[...], bcb_ref[...], bwax_ref[...], bbax_ref[...],
                 blam_ref[...], conv_slab, a_slab, b_slab, h_carry)
    yield
    y_a = yield from _gmlp(_dot(h, win_ref[:, 0:off_b]), alng_ref[...], alnb_ref[...], wm_s[...], absx_ref[...])
    yield
    clb = clb_ref[...]
    e = jnp.exp(clb - jnp.max(clb, axis=0, keepdims=True))
    sm = e / jnp.sum(e, axis=0, keepdims=True)
    lb = jnp.zeros((1, W_GRP), F32)
    for j in range(1, layer + 1):
        lb = lb + sm[j:j + 1, :]
    y_c = yield from _hgrn2(_dot(h, win_ref[:, off_c:off_d]), lb, cng_ref[...], state_t, c_slabs, tri_s, avg_s,
                            blk_s, head_s)
    yield
    y_d = _pool(_dot(h, win_ref[:, off_d:off_d + W_GRP]), dw_ref[...], dsc_ref[...], pool_slab, tile_start)
    yield
    mix = jnp.concatenate([y_a, y_b, y_c, y_d], axis=1).astype(BF16)
    out.append(x + _dot(mix, wout_ref[...]))


def _ffn_steps(final, x, p_tile, consts, hbuf, out):
    g2_ref, wup_ref, cw_ref, cb_ref, wdown_ref, g3_ref, wpe_ref, wpg_ref, gf_ref = consts
    d_ff = wdown_ref.shape[0]
    n_chunks = d_ff // FF_CHUNK
    h2 = _rms_norm(x, g2_ref[...]).astype(BF16)

    def up(col0):
        return _dot(h2, wup_ref[:, col0:col0 + FF_CHUNK])

    def conv(col0, hf):
        outs = []
        for c in range(FF_CHUNK // LANES):
            lanes = slice(col0 + c * LANES, col0 + (c + 1) * LANES)
            outs.append(_causal_conv_slab(hbuf.at[(col0 // LANES) + c], hf[:, c * LANES:(c + 1) * LANES],
                                          cw_ref[:, lanes], cb_ref[:, lanes]))
        return jnp.concatenate(outs, axis=1)

    acc = x
    hg_next, hv_next = up(0), up(d_ff)
    yield
    for j in range(n_chunks):
        hg, hv = hg_next, hv_next
        if j + 1 < n_chunks:
            hg_next, hv_next = up((j + 1) * FF_CHUNK), up(d_ff + (j + 1) * FF_CHUNK)
        gt = conv(j * FF_CHUNK, hg)
        val = conv(d_ff + j * FF_CHUNK, hv)
        act = (_gelu_tanh(gt) * val).astype(BF16)
        acc = acc + _dot(act, wdown_ref[j * FF_CHUNK:(j + 1) * FF_CHUNK, :])
        yield
    x = acc
    rstd = lax.rsqrt(jnp.mean(x * x, axis=-1, keepdims=True) + EPS)
    pe = _dot(p_tile.astype(BF16), wpe_ref[...])
    yield
    gate = jax.nn.sigmoid(_dot((x * rstd * g3_ref[...]).astype(BF16), wpg_ref[...]))
    x = x + pe * gate
    if final:
        x = _rms_norm(x, gf_ref[...])
    out.append(x)


N_MIXER_CONSTS = 16
N_FFN_CONSTS = 9
PHASE_ORDER = "fm" * 5 + "ffmm" + "fm" * 2 + "ffmmfffm"


def _layer_kernel(layer, final, n_tiles, x_ref, p_ref, *refs):
    mixer_consts = refs[:N_MIXER_CONSTS]
    ffn_consts = refs[N_MIXER_CONSTS:N_MIXER_CONSTS + N_FFN_CONSTS]
    o_ref = refs[N_MIXER_CONSTS + N_FFN_CONSTS]
    (conv_slab, a_slab, b_slab, h_carry, state_t, pool_slab, c_slabs, wm_s, tri_s, avg_s, blk_s, head_s, hbuf,
     x1_buf) = refs[N_MIXER_CONSTS + N_FFN_CONSTS + 1:]
    i = pl.program_id(0)
    s = i % n_tiles

    @pl.when(i == 0)
    def _():
        x1_buf[...] = jnp.zeros_like(x1_buf)
        _init_constants(mixer_consts[4][...], wm_s, tri_s, avg_s, blk_s, head_s)

    @pl.when(s == 0)
    def _():
        conv_slab[:, 0:CONV_HIST, :] = jnp.zeros((conv_slab.shape[0], CONV_HIST, LANES), F32)
        pool_slab[:, 0:POOL_HIST, :] = jnp.zeros((pool_slab.shape[0], POOL_HIST, LANES), F32)
        h_carry[...] = jnp.zeros_like(h_carry)
        state_t[...] = jnp.zeros_like(state_t)

    @pl.when(jnp.maximum(i - 1, 0) % n_tiles == 0)
    def _():
        hbuf[:, 0:CONV_HIST, :] = jnp.zeros((hbuf.shape[0], CONV_HIST, LANES), F32)

    ts = x_ref.shape[0]
    mixer_out, ffn_out = [], []
    gens = [
        _ffn_steps(final, x1_buf[(i + 1) % 2], p_ref[...], ffn_consts, hbuf, ffn_out),
        _mixer_steps(layer, s * ts, x_ref[...], mixer_consts,
                     (conv_slab, a_slab, b_slab, h_carry, state_t, pool_slab, c_slabs, wm_s, tri_s, avg_s, blk_s,
                      head_s), mixer_out),
    ]
    ffn_gen, mixer_gen = gens
    phase = {"f": ffn_gen, "m": mixer_gen}
    for ch in PHASE_ORDER + "fm" * 4:
        try:
            next(phase[ch])
        except StopIteration:
            pass
    assert mixer_out and ffn_out
    x1_buf[i % 2] = mixer_out[0]
    o_ref[...] = ffn_out[0]


def _layer_spec(arr, layer):
    zeros = (0,) * (arr.ndim - 1)
    return pl.BlockSpec((None,) + arr.shape[1:], lambda i: (layer,) + zeros, pipeline_mode=pl.Buffered(1))


def _params():
    return pltpu.CompilerParams(dimension_semantics=("arbitrary",), vmem_limit_bytes=VMEM_LIMIT)


def _layer_call(layer, final, x, p, mixer_consts, ffn_consts):
    bsz, seq, d = x.shape
    n_tiles = seq // TS
    n_steps = bsz * n_tiles
    d_ff = ffn_consts[4].shape[1]
    consts = tuple(mixer_consts) + tuple(ffn_consts)
    assert len(mixer_consts) == N_MIXER_CONSTS and len(ffn_consts) == N_FFN_CONSTS

    def mixer_tile(i):
        t = jnp.minimum(i, n_steps - 1)
        return t // n_tiles, t % n_tiles

    def ffn_tile(i):
        t = jnp.maximum(i - 1, 0)
        return t // n_tiles, t % n_tiles

    return pl.pallas_call(
        functools.partial(_layer_kernel, layer, final, n_tiles),
        out_shape=jax.ShapeDtypeStruct(x.shape, x.dtype),
        grid=(n_steps + 1,),
        in_specs=[pl.BlockSpec((None, TS, d), lambda i: mixer_tile(i) + (0,)),
                  pl.BlockSpec((None, None, TS, p.shape[-1]), lambda i: (layer,) + ffn_tile(i) + (0,))]
        + [_layer_spec(c, layer) for c in consts],
        out_specs=pl.BlockSpec((None, TS, d), lambda i: ffn_tile(i) + (0,)),
        scratch_shapes=[
            pltpu.VMEM((W_GRP // LANES, CONV_HIST + TS, LANES), F32),
            pltpu.VMEM((W_GRP // LANES, SUBLANES * SCAN_PITCH, LANES), F32),
            pltpu.VMEM((W_GRP // LANES, SUBLANES * SCAN_PITCH, LANES), F32),
            pltpu.VMEM((SUBLANES, W_GRP), F32),
            pltpu.VMEM((W_GRP, W_GRP), F32),
            pltpu.VMEM((W_GRP // LANES, POOL_HIST + TS, LANES), F32),
            pltpu.VMEM((N_HEADS, TS + SUBLANES, LANES), F32),
            pltpu.VMEM((GMLP_CHUNK, N_HEADS * GMLP_CHUNK), BF16),
            pltpu.VMEM((TS, TS), BF16),
            pltpu.VMEM((W_GRP, W_GRP), BF16),
            pltpu.VMEM((len(_LEVELS) - 1, TS, TS), F32),
            pltpu.VMEM((W_GRP, W_GRP), F32),
            pltpu.VMEM((2 * d_ff // LANES, CONV_HIST + TS, LANES), F32),
            pltpu.VMEM((2, TS, d), F32),
        ],
        compiler_params=_params(),
        name=f"layer_l{layer}",
    )(x, p, *consts)


def _block_diag(w):
    depth, n_h, d_in, d_out = w.shape
    eye = jnp.eye(n_h, dtype=w.dtype)
    return jnp.einsum("lhde,hg->lhdge", w, eye).reshape(depth, n_h * d_in, n_h * d_out)


def _rows(v):
    return v.reshape(v.shape[0], 1, -1).astype(F32)


def kernel(x, p, norm1_g, w_in, a_ln_g, a_ln_b, a_ws, a_bs, b_conv_w, b_conv_b, b_wa, b_ba, b_wx, b_bx, b_lam, c_lb, c_norm_g, d_w, d_scale, w_out, norm2_g, w_up, ffn_conv_w, ffn_conv_b, w_down, norm3_g, w_pe, w_pg, final_g):
    depth = w_in.shape[0]
    d_ff = w_down.shape[1]
    assert x.shape[1] % TS == 0 and d_ff % FF_CHUNK == 0
    mixer_consts = (
        _rows(norm1_g),
        w_in.astype(BF16),
        _rows(a_ln_g), _rows(a_ln_b),
        a_ws.transpose(0, 2, 1, 3).reshape(depth, GMLP_CHUNK, N_HEADS * GMLP_CHUNK).astype(F32),
        jnp.repeat(a_bs.transpose(0, 2, 1), HEAD_DIM, axis=2).astype(F32),
        b_conv_w.astype(F32), _rows(b_conv_b),
        jnp.concatenate([_block_diag(b_wa), _block_diag(b_wx)], axis=2).astype(BF16),
        _rows(jnp.concatenate([b_ba, b_bx], axis=1)),
        _rows(b_lam),
        jnp.broadcast_to(c_lb.astype(F32), (depth,) + c_lb.shape),
        _rows(jnp.tile(c_norm_g, (1, N_HEADS))),
        _block_diag(d_w).astype(BF16), _rows(d_scale),
        w_out.astype(BF16),
    )
    ffn_consts = (
        _rows(norm2_g),
        w_up.astype(BF16),
        ffn_conv_w.astype(F32), _rows(ffn_conv_b),
        w_down.astype(BF16),
        _rows(norm3_g),
        w_pe.astype(BF16), w_pg.astype(BF16),
        jnp.broadcast_to(final_g.astype(F32).reshape(1, 1, -1), (depth, 1, final_g.shape[0])),
    )
    for l in range(depth):
        x = _layer_call(l, l == depth - 1, x, p, mixer_consts, ffn_consts)
    return x
```

```python
import functools

import jax
import jax.numpy as jnp
from jax import lax
from jax.experimental import pallas as pl
from jax.experimental.pallas import tpu as pltpu

F32 = jnp.float32
BF16 = jnp.bfloat16

EPS = 1e-6
N_HEADS = 4
HEAD_DIM = 64
W_GRP = N_HEADS * HEAD_DIM
GMLP_CHUNK = 128
RGLRU_C = 8.0
POOL_WINDOWS = (2, 4, 8, 16)
POOL_HIST = 16
CONV_HIST = 8
LANES = 128
SUBLANES = 8
TS = 256
SCAN_SEG = TS // SUBLANES
SCAN_PITCH = SCAN_SEG + 1
FF_CHUNK = 256
LOG2E = 1.4426950408889634
VMEM_LIMIT = 56 * 1024 * 1024


def _dot(a, b):
    return jnp.dot(a, b, preferred_element_type=F32)


def _dot_nt(a, b):
    return lax.dot_general(a, b, (((1,), (1,)), ((), ())), preferred_element_type=F32)


def _dot_tn(a, b):
    return lax.dot_general(a, b, (((0,), (0,)), ((), ())), preferred_element_type=F32)


def _rms_norm(x, g):
    return x * lax.rsqrt(jnp.mean(x * x, axis=-1, keepdims=True) + EPS) * g


def _iota(shape, axis):
    return lax.broadcasted_iota(jnp.int32, shape, axis)


def _gelu_tanh(x):
    c = 0.7978845608028654
    inner = x * (c + (c * 0.044715) * (x * x))
    return x * (0.5 + 0.5 * jnp.tanh(inner))


def _split3(x):
    p1 = x.astype(BF16)
    r1 = x - p1.astype(F32)
    p2 = r1.astype(BF16)
    p3 = (r1 - p2.astype(F32)).astype(BF16)
    return p1, p2, p3


def _init_constants(ws_cat, wm_s, tri_s, avg_s, blk_s, head_s):
    ts = tri_s.shape[0]
    t_idx = _iota(ws_cat.shape, 0)
    s_idx = _iota(ws_cat.shape, 1) % GMLP_CHUNK
    wm_s[...] = jnp.where(s_idx <= t_idx, ws_cat, 0.0).astype(BF16)
    tt = _iota((ts, ts), 0)
    ss = _iota((ts, ts), 1)
    tri_s[...] = (ss <= tt).astype(BF16)
    same_head = (tt // HEAD_DIM) == (ss // HEAD_DIM)
    avg_s[...] = jnp.where(same_head, 1.0 / HEAD_DIM, 0.0).astype(BF16)
    head_s[...] = jnp.where(same_head, 1.0, 0.0)
    for n, (outer, _) in enumerate(_LEVELS[1:]):
        blk_s[n] = jnp.where((tt // outer) == (ss // outer), 1.0, 0.0)


def _gmlp(z_a, ln_g, ln_b, wm, bs_x):
    ts = z_a.shape[0]
    ab = _gelu_tanh(z_a)
    u, v = ab[:, :W_GRP], ab[:, W_GRP:]
    mu = jnp.mean(v, axis=-1, keepdims=True)
    var = jnp.mean(jnp.square(v - mu), axis=-1, keepdims=True)
    rstd = lax.rsqrt(var + EPS)
    yield
    vn = (v - mu) * rstd * ln_g + ln_b
    head = _iota((GMLP_CHUNK, W_GRP), 1) // HEAD_DIM
    outs = []
    for c in range(ts // GMLP_CHUNK):
        vc = vn[c * GMLP_CHUNK:(c + 1) * GMLP_CHUNK]
        rhs = jnp.concatenate([jnp.where(head == h, vc, 0.0) for h in range(N_HEADS)], axis=0).astype(BF16)
        sv = _dot(wm, rhs) + bs_x
        outs.append(u[c * GMLP_CHUNK:(c + 1) * GMLP_CHUNK] * sv)
    return jnp.concatenate(outs, axis=0)


def _causal_conv_slab(slab, x_tile, w, b):
    ts = x_tile.shape[0]
    k_width = w.shape[0]
    slab[CONV_HIST:CONV_HIST + ts, :] = x_tile
    y = b + w[k_width - 1:k_width] * x_tile
    for k in range(k_width - 1):
        shift = k_width - 1 - k
        y = y + w[k:k + 1] * slab[CONV_HIST - shift:CONV_HIST - shift + ts, :]
    slab[0:CONV_HIST, :] = slab[ts:ts + CONV_HIST, :]
    return y


def _rglru(z_b, conv_w, conv_b, wax, bax, lam, conv_slab, a_slab, b_slab, h_carry):
    ts = z_b.shape[0]
    n_tiles = W_GRP // LANES
    xb, gb = z_b[:, :W_GRP], z_b[:, W_GRP:]
    xc = jnp.concatenate(
        [_causal_conv_slab(conv_slab.at[c], xb[:, c * LANES:(c + 1) * LANES],
                           conv_w[:, c * LANES:(c + 1) * LANES], conv_b[:, c * LANES:(c + 1) * LANES])
         for c in range(n_tiles)], axis=1)
    ri = jax.nn.sigmoid(_dot(xc.astype(BF16), wax) + bax)
    r, i = ri[:, :W_GRP], ri[:, W_GRP:]
    log_a = (-RGLRU_C) * r * jax.nn.softplus(-lam)
    a = jnp.exp(log_a)
    th = jnp.tanh(log_a)
    mult = jnp.sqrt(-2.0 * th / (1.0 - th))
    bt = mult * (i * xc)
    assert ts == SUBLANES * SCAN_SEG
    hs = []
    for c in range(n_tiles):
        lanes = slice(c * LANES, (c + 1) * LANES)
        for sg in range(SUBLANES):
            rows = slice(sg * SCAN_SEG, (sg + 1) * SCAN_SEG)
            a_slab[c, sg * SCAN_PITCH:sg * SCAN_PITCH + SCAN_SEG, :] = a[rows, lanes]
            b_slab[c, sg * SCAN_PITCH:sg * SCAN_PITCH + SCAN_SEG, :] = bt[rows, lanes]
        a_cum = h_loc = None
        for j in range(SCAN_SEG):
            col = pl.ds(j, SUBLANES, stride=SCAN_PITCH)
            a_j, b_j = a_slab[c, col, :], b_slab[c, col, :]
            if j == 0:
                a_cum, h_loc = a_j, b_j
            else:
                h_loc = a_j * h_loc + b_j
                a_cum = a_j * a_cum
            a_slab[c, col, :] = a_cum
            b_slab[c, col, :] = h_loc
        carry = h_carry[0:1, lanes]
        segs = []
        for sg in range(SUBLANES):
            rows = slice(sg * SCAN_PITCH, sg * SCAN_PITCH + SCAN_SEG)
            segs.append(b_slab[c, rows, :] + a_slab[c, rows, :] * carry)
            carry = a_cum[sg:sg + 1] * carry + h_loc[sg:sg + 1]
        h_carry[:, lanes] = jnp.broadcast_to(carry, (h_carry.shape[0], LANES))
        hs.append(jnp.concatenate(segs, axis=0))
    h = jnp.concatenate(hs, axis=1)
    return h * _gelu_tanh(gb)


_LEVELS = ((256, 64), (64, 16), (16, 4), (4, 1))


def _level_slots(inner):
    slots = (0, 1, 2, 3) if inner == 1 else (1, 2, 3, None)
    return (slots[0:2], slots[2:4])


def _hgrn2_head_att(qd, kd, b_slab, blk_s):
    ts = qd.shape[0]
    lo1 = _iota((1, LANES), 1) < HEAD_DIM
    att = None
    for n_level, (outer, inner) in enumerate(_LEVELS[:3]):
        piece = max(inner, 2 * SUBLANES)
        lo = jnp.broadcast_to(lo1, (piece, LANES))
        pos = (_iota((piece, LANES), 0) % outer) // inner
        qcols, kcols = [], []
        for ja, jb in _level_slots(inner):
            jsel = jnp.where(lo, ja, -1 if jb is None else jb)
            qp, kp = [], []
            for base in range(0, ts, outer):
                ref = b_slab[base + ja * inner:base + ja * inner + 1, :]
                if jb is not None:
                    ref = jnp.where(lo1, ref, b_slab[base + jb * inner:base + jb * inner + 1, :])
                for r0 in range(base, base + outer, piece):
                    p0 = (r0 - base) // inner
                    p1 = (r0 - base + piece - 1) // inner
                    js = [j for j in (ja, jb) if j is not None]
                    rows = slice(r0, r0 + piece)
                    pos_r = pos + p0 if inner >= piece else pos
                    if any(p0 <= j <= p1 for j in js):
                        qp.append(jnp.where(pos_r == jsel, qd[rows] * jnp.exp2(b_slab[rows, :] - ref), 0.0))
                    else:
                        qp.append(jnp.zeros((piece, LANES), F32))
                    if any(p0 < j for j in js):
                        kp.append(jnp.where(pos_r < jsel, kd[rows] * jnp.exp2(ref - b_slab[rows, :]), 0.0))
                    else:
                        kp.append(jnp.zeros((piece, LANES), F32))
            qcols.append(jnp.concatenate(qp, axis=0))
            kcols.append(jnp.concatenate(kp, axis=0))
        a_l = _dot_nt(jnp.concatenate(qcols, axis=1).astype(BF16), jnp.concatenate(kcols, axis=1).astype(BF16))
        if outer < ts:
            a_l = a_l * blk_s[n_level - 1]
        att = a_l if att is None else att + a_l

    outer, inner = _LEVELS[3]
    assert inner == 1 and outer == 4
    lo = jnp.broadcast_to(lo1, (ts, LANES))
    m = _iota((ts, LANES), 0) % outer
    b0 = b_slab[0:ts, :]
    e = [None] + [jnp.exp2(b_slab[d:ts + d, :] - b0) for d in range(1, outer)]
    qcols, kcols = [], []
    for ja, jb in _level_slots(inner):
        jsel = jnp.where(lo, ja, jb)
        qcols.append(jnp.where(m == jsel, qd, 0.0))
        d = jsel - m
        f = jnp.where(d == 0, 1.0, 0.0)
        for dd in range(1, outer):
            f = jnp.where(d == dd, e[dd], f)
        kcols.append(kd * f)
    a_l = _dot_nt(jnp.concatenate(qcols, axis=1).astype(BF16), jnp.concatenate(kcols, axis=1).astype(BF16))
    return att + a_l * blk_s[len(_LEVELS) - 2]


def _dup_heads(x):
    out = []
    lo_half = _iota((x.shape[0], LANES), 1) < HEAD_DIM
    for pair in range(N_HEADS // 2):
        xp = x[:, pair * LANES:(pair + 1) * LANES]
        rot = pltpu.roll(xp, HEAD_DIM, 1)
        out.append(jnp.where(lo_half, xp, rot))
        out.append(jnp.where(lo_half, rot, xp))
    return out


def _hgrn2(z_c, lb, norm_g, state_t, b_slabs, tri_s, avg_s, blk_s, head_s):
    ts = z_c.shape[0]
    q, f, v, g = (z_c[:, n * W_GRP:(n + 1) * W_GRP] for n in range(4))
    qf = q * jax.nn.sigmoid(q)
    fgate = lb + (1.0 - lb) * jax.nn.sigmoid(f)
    log2f = jnp.log(fgate) * LOG2E
    kf = 1.0 - fgate
    tri = tri_s[...]
    p1, p2, p3 = _split3(log2f)
    bb = _dot(tri, p1) + _dot(tri, p2) + _dot(tri, p3)
    yield

    head = _iota((ts, W_GRP), 1) // HEAD_DIM
    v16 = v.astype(BF16)
    qh, kh = _dup_heads(qf), _dup_heads(kf)
    for h, bh in enumerate(_dup_heads(bb)):
        b_slabs[h, 0:ts, :] = bh
        b_slabs[h, ts:ts + SUBLANES, :] = jnp.zeros((SUBLANES, LANES), F32)
    yield
    o = _dot_nt((qf * jnp.exp2(bb)).astype(BF16), state_t[...].astype(BF16))
    for h in range(N_HEADS):
        att = _hgrn2_head_att(qh[h], kh[h], b_slabs.at[h], blk_s)
        o = o + _dot(att.astype(BF16), jnp.where(head == h, v16, jnp.zeros_like(v16)))
        yield

    bl = bb[ts - 1:ts, :]
    k_hat = (kf * jnp.exp2(bl - bb)).astype(BF16)
    upd = _dot_tn(v16, k_hat)
    state_t[...] = state_t[...] * jnp.exp2(bl) + upd * head_s[...]

    avg = avg_s[...]
    oo = o * o
    o_hi = oo.astype(BF16)
    o_lo = (oo - o_hi.astype(F32)).astype(BF16)
    ms = _dot(o_hi, avg) + _dot(o_lo, avg)
    o = o * lax.rsqrt(ms + EPS) * norm_g
    return o * (g * jax.nn.sigmoid(g))


def _pool(xd, dw, scale, pool_slab, tile_start):
    ts = xd.shape[0]
    lo_half = _iota((ts, LANES), 1) < HEAD_DIM
    pos = (tile_start + _iota((ts, LANES), 0) + 1).astype(F32)
    pooled = []
    for c in range(W_GRP // LANES):
        slab = pool_slab.at[c]
        x_tile = xd[:, c * LANES:(c + 1) * LANES]
        slab[POOL_HIST:POOL_HIST + ts, :] = x_tile
        w_lo, w_hi = POOL_WINDOWS[2 * c], POOL_WINDOWS[2 * c + 1]
        if w_lo % SUBLANES == 0 and w_hi == 2 * w_lo:
            acc = slab[POOL_HIST - w_lo:POOL_HIST + ts, :]
            for k in range(1, w_lo):
                acc = acc + slab[POOL_HIST - w_lo - k:POOL_HIST + ts - k, :]
            s_lo = acc[w_lo:]
            s_hi = s_lo + acc[:ts]
        else:
            acc = x_tile
            for k in range(1, w_lo):
                acc = acc + slab[POOL_HIST - k:POOL_HIST - k + ts, :]
            s_lo = acc
            for k in range(w_lo, w_hi):
                acc = acc + slab[POOL_HIST - k:POOL_HIST - k + ts, :]
            s_hi = acc
        slab[0:POOL_HIST, :] = slab[ts:ts + POOL_HIST, :]
        wsum = jnp.where(lo_half, s_lo, s_hi)
        wlen = jnp.where(lo_half, float(w_lo), float(w_hi))
        pooled.append(wsum / jnp.minimum(pos, wlen) - x_tile)
    return _dot(jnp.concatenate(pooled, axis=1).astype(BF16), dw) * scale


def _mixer_steps(layer, tile_start, x, consts, scratch, out):
    (g1_ref, win_ref, alng_ref, alnb_ref, aws_ref, absx_ref, bcw_ref, bcb_ref, bwax_ref, bbax_ref, blam_ref,
     clb_ref, cng_ref, dw_ref, dsc_ref, wout_ref) = consts
    conv_slab, a_slab, b_slab, h_carry, state_t, pool_slab, c_slabs, wm_s, tri_s, avg_s, blk_s, head_s = scratch
    h = _rms_norm(x, g1_ref[...]).astype(BF16)
    off_b, off_c, off_d = 2 * W_GRP, 4 * W_GRP, 8 * W_GRP

    y_a = yield from _gmlp(_dot(h, win_ref[:, 0:off_b]), alng_ref[...], alnb_ref[...], wm_s[...], absx_ref[...])
    yield
    y_b = _rglru(_dot(h, win_ref[:, off_b:off_c]), bcw_ref[...], bcb_ref[...], bwax_ref[...], bbax_ref[...],
                 blam_ref[...], conv_slab, a_slab, b_slab, h_carry)
    yield
    clb = clb_ref[...]
    e = jnp.exp(clb - jnp.max(clb, axis=0, keepdims=True))
    sm = e / jnp.sum(e, axis=0, keepdims=True)
    lb = jnp.zeros((1, W_GRP), F32)
    for j in range(1, layer + 1):
        lb = lb + sm[j:j + 1, :]
    y_c = yield from _hgrn2(_dot(h, win_ref[:, off_c:off_d]), lb, cng_ref[...], state_t, c_slabs, tri_s, avg_s,
                            blk_s, head_s)
    yield
    y_d = _pool(_dot(h, win_ref[:, off_d:off_d + W_GRP]), dw_ref[...], dsc_ref[...], pool_slab, tile_start)
    yield
    mix = jnp.concatenate([y_a, y_b, y_c, y_d], axis=1).astype(BF16)
    out.append(x + _dot(mix, wout_ref[...]))


def _ffn_steps(final, x, p_tile, consts, hbuf, out):
    g2_ref, wup_ref, cw_ref, cb_ref, wdown_ref, g3_ref, wpe_ref, wpg_ref, gf_ref = consts
    d_ff = wdown_ref.shape[0]
    n_chunks = d_ff // FF_CHUNK
    pe = _dot(p_tile.astype(BF16), wpe_ref[...])
    h2 = _rms_norm(x, g2_ref[...]).astype(BF16)

    def up(col0):
        return _dot(h2, wup_ref[:, col0:col0 + FF_CHUNK])

    def conv(col0, hf):
        outs = []
        for c in range(FF_CHUNK // LANES):
            lanes = slice(col0 + c * LANES, col0 + (c + 1) * LANES)
            outs.append(_causal_conv_slab(hbuf.at[(col0 // LANES) + c], hf[:, c * LANES:(c + 1) * LANES],
                                          cw_ref[:, lanes], cb_ref[:, lanes]))
        return jnp.concatenate(outs, axis=1)

    acc = x
    hg_next, hv_next = up(0), up(d_ff)
    yield
    for j in range(n_chunks):
        hg, hv = hg_next, hv_next
        if j + 1 < n_chunks:
            hg_next, hv_next = up((j + 1) * FF_CHUNK), up(d_ff + (j + 1) * FF_CHUNK)
        gt = conv(j * FF_CHUNK, hg)
        val = conv(d_ff + j * FF_CHUNK, hv)
        act = (_gelu_tanh(gt) * val).astype(BF16)
        acc = acc + _dot(act, wdown_ref[j * FF_CHUNK:(j + 1) * FF_CHUNK, :])
        yield
    x = acc
    rstd = lax.rsqrt(jnp.mean(x * x, axis=-1, keepdims=True) + EPS)
    yield
    gate = jax.nn.sigmoid(_dot((x * rstd * g3_ref[...]).astype(BF16), wpg_ref[...]))
    x = x + pe * gate
    if final:
        x = _rms_norm(x, gf_ref[...])
    out.append(x)


N_MIXER_CONSTS = 16
N_FFN_CONSTS = 9
PHASE_ORDER = "fm" * 9 + "ffmmfffm"


def _layer_kernel(layer, final, n_tiles, x_ref, p_ref, *refs):
    mixer_consts = refs[:N_MIXER_CONSTS]
    ffn_consts = refs[N_MIXER_CONSTS:N_MIXER_CONSTS + N_FFN_CONSTS]
    o_ref = refs[N_MIXER_CONSTS + N_FFN_CONSTS]
    (conv_slab, a_slab, b_slab, h_carry, state_t, pool_slab, c_slabs, wm_s, tri_s, avg_s, blk_s, head_s, hbuf,
     x1_buf) = refs[N_MIXER_CONSTS + N_FFN_CONSTS + 1:]
    i = pl.program_id(0)
    s = i % n_tiles

    @pl.when(i == 0)
    def _():
        x1_buf[...] = jnp.zeros_like(x1_buf)
        _init_constants(mixer_consts[4][...], wm_s, tri_s, avg_s, blk_s, head_s)

    @pl.when(s == 0)
    def _():
        conv_slab[:, 0:CONV_HIST, :] = jnp.zeros((conv_slab.shape[0], CONV_HIST, LANES), F32)
        pool_slab[:, 0:POOL_HIST, :] = jnp.zeros((pool_slab.shape[0], POOL_HIST, LANES), F32)
        h_carry[...] = jnp.zeros_like(h_carry)
        state_t[...] = jnp.zeros_like(state_t)

    @pl.when(jnp.maximum(i - 1, 0) % n_tiles == 0)
    def _():
        hbuf[:, 0:CONV_HIST, :] = jnp.zeros((hbuf.shape[0], CONV_HIST, LANES), F32)

    ts = x_ref.shape[0]
    mixer_out, ffn_out = [], []
    gens = [
        _ffn_steps(final, x1_buf[(i + 1) % 2], p_ref[...], ffn_consts, hbuf, ffn_out),
        _mixer_steps(layer, s * ts, x_ref[...], mixer_consts,
                     (conv_slab, a_slab, b_slab, h_carry, state_t, pool_slab, c_slabs, wm_s, tri_s, avg_s, blk_s,
                      head_s), mixer_out),
    ]
    ffn_gen, mixer_gen = gens
    phase = {"f": ffn_gen, "m": mixer_gen}
    for ch in PHASE_ORDER + "fm" * 4:
        try:
            next(phase[ch])
        except StopIteration:
            pass
    assert mixer_out and ffn_out
    x1_buf[i % 2] = mixer_out[0]
    o_ref[...] = ffn_out[0]


def _layer_spec(arr, layer):
    zeros = (0,) * (arr.ndim - 1)
    return pl.BlockSpec((None,) + arr.shape[1:], lambda i: (layer,) + zeros, pipeline_mode=pl.Buffered(1))


def _params():
    return pltpu.CompilerParams(dimension_semantics=("arbitrary",), vmem_limit_bytes=VMEM_LIMIT)


def _layer_call(layer, final, x, p, mixer_consts, ffn_consts):
    bsz, seq, d = x.shape
    n_tiles = seq // TS
    n_steps = bsz * n_tiles
    d_ff = ffn_consts[4].shape[1]
    consts = tuple(mixer_consts) + tuple(ffn_consts)
    assert len(mixer_consts) == N_MIXER_CONSTS and len(ffn_consts) == N_FFN_CONSTS

    def mixer_tile(i):
        t = jnp.minimum(i, n_steps - 1)
        return t // n_tiles, t % n_tiles

    def ffn_tile(i):
        t = jnp.maximum(i - 1, 0)
        return t // n_tiles, t % n_tiles

    return pl.pallas_call(
        functools.partial(_layer_kernel, layer, final, n_tiles),
        out_shape=jax.ShapeDtypeStruct(x.shape, x.dtype),
        grid=(n_steps + 1,),
        in_specs=[pl.BlockSpec((None, TS, d), lambda i: mixer_tile(i) + (0,)),
                  pl.BlockSpec((None, None, TS, p.shape[-1]), lambda i: (layer,) + ffn_tile(i) + (0,))]
        + [_layer_spec(c, layer) for c in consts],
        out_specs=pl.BlockSpec((None, TS, d), lambda i: ffn_tile(i) + (0,)),
        scratch_shapes=[
            pltpu.VMEM((W_GRP // LANES, CONV_HIST + TS, LANES), F32),
            pltpu.VMEM((W_GRP // LANES, SUBLANES * SCAN_PITCH, LANES), F32),
            pltpu.VMEM((W_GRP // LANES, SUBLANES * SCAN_PITCH, LANES), F32),
            pltpu.VMEM((SUBLANES, W_GRP), F32),
            pltpu.VMEM((W_GRP, W_GRP), F32),
            pltpu.VMEM((W_GRP // LANES, POOL_HIST + TS, LANES), F32),
            pltpu.VMEM((N_HEADS, TS + SUBLANES, LANES), F32),
            pltpu.VMEM((GMLP_CHUNK, N_HEADS * GMLP_CHUNK), BF16),
            pltpu.VMEM((TS, TS), BF16),
            pltpu.VMEM((W_GRP, W_GRP), BF16),
            pltpu.VMEM((len(_LEVELS) - 1, TS, TS), F32),
            pltpu.VMEM((W_GRP, W_GRP), F32),
            pltpu.VMEM((2 * d_ff // LANES, CONV_HIST + TS, LANES), F32),
            pltpu.VMEM((2, TS, d), F32),
        ],
        compiler_params=_params(),
        name=f"layer_l{layer}",
    )(x, p, *consts)


def _block_diag(w):
    depth, n_h, d_in, d_out = w.shape
    eye = jnp.eye(n_h, dtype=w.dtype)
    return jnp.einsum("lhde,hg->lhdge", w, eye).reshape(depth, n_h * d_in, n_h * d_out)


def _rows(v):
    return v.reshape(v.shape[0], 1, -1).astype(F32)


def kernel(x, p, norm1_g, w_in, a_ln_g, a_ln_b, a_ws, a_bs, b_conv_w, b_conv_b, b_wa, b_ba, b_wx, b_bx, b_lam, c_lb, c_norm_g, d_w, d_scale, w_out, norm2_g, w_up, ffn_conv_w, ffn_conv_b, w_down, norm3_g, w_pe, w_pg, final_g):
    depth = w_in.shape[0]
    d_ff = w_down.shape[1]
    assert x.shape[1] % TS == 0 and d_ff % FF_CHUNK == 0
    mixer_consts = (
        _rows(norm1_g),
        w_in.astype(BF16),
        _rows(a_ln_g), _rows(a_ln_b),
        a_ws.transpose(0, 2, 1, 3).reshape(depth, GMLP_CHUNK, N_HEADS * GMLP_CHUNK).astype(F32),
        jnp.repeat(a_bs.transpose(0, 2, 1), HEAD_DIM, axis=2).astype(F32),
        b_conv_w.astype(F32), _rows(b_conv_b),
        jnp.concatenate([_block_diag(b_wa), _block_diag(b_wx)], axis=2).astype(BF16),
        _rows(jnp.concatenate([b_ba, b_bx], axis=1)),
        _rows(b_lam),
        jnp.broadcast_to(c_lb.astype(F32), (depth,) + c_lb.shape),
        _rows(jnp.tile(c_norm_g, (1, N_HEADS))),
        _block_diag(d_w).astype(BF16), _rows(d_scale),
        w_out.astype(BF16),
    )
    ffn_consts = (
        _rows(norm2_g),
        w_up.astype(BF16),
        ffn_conv_w.astype(F32), _rows(ffn_conv_b),
        w_down.astype(BF16),
        _rows(norm3_g),
        w_pe.astype(BF16), w_pg.astype(BF16),
        jnp.broadcast_to(final_g.astype(F32).reshape(1, 1, -1), (depth, 1, final_g.shape[0])),
    )
    for l in range(depth):
        x = _layer_call(l, l == depth - 1, x, p, mixer_consts, ffn_consts)
    return x
```

```python
import functools

import jax
import jax.numpy as jnp
from jax import lax
from jax.experimental import pallas as pl
from jax.experimental.pallas import tpu as pltpu

F32 = jnp.float32
BF16 = jnp.bfloat16

EPS = 1e-6
N_HEADS = 4
HEAD_DIM = 64
W_GRP = N_HEADS * HEAD_DIM
GMLP_CHUNK = 128
RGLRU_C = 8.0
POOL_WINDOWS = (2, 4, 8, 16)
POOL_HIST = 16
CONV_HIST = 8
LANES = 128
SUBLANES = 8
TS = 256
SCAN_SEG = TS // SUBLANES
SCAN_PITCH = SCAN_SEG + 1
FF_CHUNK = 256
LOG2E = 1.4426950408889634
VMEM_LIMIT = 56 * 1024 * 1024


def _dot(a, b):
    return jnp.dot(a, b, preferred_element_type=F32)


def _dot_nt(a, b):
    return lax.dot_general(a, b, (((1,), (1,)), ((), ())), preferred_element_type=F32)


def _dot_tn(a, b):
    return lax.dot_general(a, b, (((0,), (0,)), ((), ())), preferred_element_type=F32)


def _rms_norm(x, g):
    return x * lax.rsqrt(jnp.mean(x * x, axis=-1, keepdims=True) + EPS) * g


def _iota(shape, axis):
    return lax.broadcasted_iota(jnp.int32, shape, axis)


def _gelu_tanh(x):
    c = 0.7978845608028654
    inner = x * (c + (c * 0.044715) * (x * x))
    return x * (0.5 + 0.5 * jnp.tanh(inner))


def _split3(x):
    p1 = x.astype(BF16)
    r1 = x - p1.astype(F32)
    p2 = r1.astype(BF16)
    p3 = (r1 - p2.astype(F32)).astype(BF16)
    return p1, p2, p3


def _init_constants(ws_cat, wm_s, tri_s, avg_s, blk_s, head_s):
    ts = tri_s.shape[0]
    t_idx = _iota(ws_cat.shape, 0)
    s_idx = _iota(ws_cat.shape, 1) % GMLP_CHUNK
    wm_s[...] = jnp.where(s_idx <= t_idx, ws_cat, 0.0).astype(BF16)
    tt = _iota((ts, ts), 0)
    ss = _iota((ts, ts), 1)
    tri_s[...] = (ss <= tt).astype(BF16)
    same_head = (tt // HEAD_DIM) == (ss // HEAD_DIM)
    avg_s[...] = jnp.where(same_head, 1.0 / HEAD_DIM, 0.0).astype(BF16)
    head_s[...] = jnp.where(same_head, 1.0, 0.0)
    for n, (outer, _) in enumerate(_LEVELS[1:]):
        blk_s[n] = jnp.where((tt // outer) == (ss // outer), 1.0, 0.0)


def _gmlp(z_a, ln_g, ln_b, wm, bs_x):
    ts = z_a.shape[0]
    ab = _gelu_tanh(z_a)
    u, v = ab[:, :W_GRP], ab[:, W_GRP:]
    mu = jnp.mean(v, axis=-1, keepdims=True)
    var = jnp.mean(jnp.square(v - mu), axis=-1, keepdims=True)
    rstd = lax.rsqrt(var + EPS)
    yield
    vn = (v - mu) * rstd * ln_g + ln_b
    head = _iota((GMLP_CHUNK, W_GRP), 1) // HEAD_DIM
    outs = []
    for c in range(ts // GMLP_CHUNK):
        vc = vn[c * GMLP_CHUNK:(c + 1) * GMLP_CHUNK]
        rhs = jnp.concatenate([jnp.where(head == h, vc, 0.0) for h in range(N_HEADS)], axis=0).astype(BF16)
        sv = _dot(wm, rhs) + bs_x
        outs.append(u[c * GMLP_CHUNK:(c + 1) * GMLP_CHUNK] * sv)
    return jnp.concatenate(outs, axis=0)


def _causal_conv_slab(slab, x_tile, w, b):
    ts = x_tile.shape[0]
    k_width = w.shape[0]
    slab[CONV_HIST:CONV_HIST + ts, :] = x_tile
    y = b + w[k_width - 1:k_width] * x_tile
    for k in range(k_width - 1):
        shift = k_width - 1 - k
        y = y + w[k:k + 1] * slab[CONV_HIST - shift:CONV_HIST - shift + ts, :]
    slab[0:CONV_HIST, :] = slab[ts:ts + CONV_HIST, :]
    return y


def _rglru(z_b, conv_w, conv_b, wax, bax, lam, conv_slab, a_slab, b_slab, h_carry):
    ts = z_b.shape[0]
    n_tiles = W_GRP // LANES
    xb, gb = z_b[:, :W_GRP], z_b[:, W_GRP:]
    xc = jnp.concatenate(
        [_causal_conv_slab(conv_slab.at[c], xb[:, c * LANES:(c + 1) * LANES],
                           conv_w[:, c * LANES:(c + 1) * LANES], conv_b[:, c * LANES:(c + 1) * LANES])
         for c in range(n_tiles)], axis=1)
    ri = jax.nn.sigmoid(_dot(xc.astype(BF16), wax) + bax)
    r, i = ri[:, :W_GRP], ri[:, W_GRP:]
    log_a = (-RGLRU_C) * r * jax.nn.softplus(-lam)
    a = jnp.exp(log_a)
    th = jnp.tanh(log_a)
    mult = jnp.sqrt(-2.0 * th / (1.0 - th))
    bt = mult * (i * xc)
    yield
    assert ts == SUBLANES * SCAN_SEG
    hs = []
    for c in range(n_tiles):
        lanes = slice(c * LANES, (c + 1) * LANES)
        for sg in range(SUBLANES):
            rows = slice(sg * SCAN_SEG, (sg + 1) * SCAN_SEG)
            a_slab[c, sg * SCAN_PITCH:sg * SCAN_PITCH + SCAN_SEG, :] = a[rows, lanes]
            b_slab[c, sg * SCAN_PITCH:sg * SCAN_PITCH + SCAN_SEG, :] = bt[rows, lanes]
        a_cum = h_loc = None
        for j in range(SCAN_SEG):
            col = pl.ds(j, SUBLANES, stride=SCAN_PITCH)
            a_j, b_j = a_slab[c, col, :], b_slab[c, col, :]
            if j == 0:
                a_cum, h_loc = a_j, b_j
            else:
                h_loc = a_j * h_loc + b_j
                a_cum = a_j * a_cum
            a_slab[c, col, :] = a_cum
            b_slab[c, col, :] = h_loc
        carry = h_carry[0:1, lanes]
        segs = []
        for sg in range(SUBLANES):
            rows = slice(sg * SCAN_PITCH, sg * SCAN_PITCH + SCAN_SEG)
            segs.append(b_slab[c, rows, :] + a_slab[c, rows, :] * carry)
            carry = a_cum[sg:sg + 1] * carry + h_loc[sg:sg + 1]
        h_carry[:, lanes] = jnp.broadcast_to(carry, (h_carry.shape[0], LANES))
        hs.append(jnp.concatenate(segs, axis=0))
    h = jnp.concatenate(hs, axis=1)
    return h * _gelu_tanh(gb)


_LEVELS = ((256, 64), (64, 16), (16, 4), (4, 1))


def _level_slots(inner):
    slots = (0, 1, 2, 3) if inner == 1 else (1, 2, 3, None)
    return (slots[0:2], slots[2:4])


def _hgrn2_head_att(qd, kd, b_slab, blk_s):
    ts = qd.shape[0]
    lo1 = _iota((1, LANES), 1) < HEAD_DIM
    att = None
    for n_level, (outer, inner) in enumerate(_LEVELS[:3]):
        piece = max(inner, 2 * SUBLANES)
        lo = jnp.broadcast_to(lo1, (piece, LANES))
        pos = (_iota((piece, LANES), 0) % outer) // inner
        qcols, kcols = [], []
        for ja, jb in _level_slots(inner):
            jsel = jnp.where(lo, ja, -1 if jb is None else jb)
            qp, kp = [], []
            for base in range(0, ts, outer):
                ref = b_slab[base + ja * inner:base + ja * inner + 1, :]
                if jb is not None:
                    ref = jnp.where(lo1, ref, b_slab[base + jb * inner:base + jb * inner + 1, :])
                for r0 in range(base, base + outer, piece):
                    p0 = (r0 - base) // inner
                    p1 = (r0 - base + piece - 1) // inner
                    js = [j for j in (ja, jb) if j is not None]
                    rows = slice(r0, r0 + piece)
                    pos_r = pos + p0 if inner >= piece else pos
                    if any(p0 <= j <= p1 for j in js):
                        qp.append(jnp.where(pos_r == jsel, qd[rows] * jnp.exp2(b_slab[rows, :] - ref), 0.0))
                    else:
                        qp.append(jnp.zeros((piece, LANES), F32))
                    if any(p0 < j for j in js):
                        kp.append(jnp.where(pos_r < jsel, kd[rows] * jnp.exp2(ref - b_slab[rows, :]), 0.0))
                    else:
                        kp.append(jnp.zeros((piece, LANES), F32))
            qcols.append(jnp.concatenate(qp, axis=0))
            kcols.append(jnp.concatenate(kp, axis=0))
        a_l = _dot_nt(jnp.concatenate(qcols, axis=1).astype(BF16), jnp.concatenate(kcols, axis=1).astype(BF16))
        if outer < ts:
            a_l = a_l * blk_s[n_level - 1]
        att = a_l if att is None else att + a_l

    outer, inner = _LEVELS[3]
    assert inner == 1 and outer == 4
    lo = jnp.broadcast_to(lo1, (ts, LANES))
    m = _iota((ts, LANES), 0) % outer
    b0 = b_slab[0:ts, :]
    e = [None] + [jnp.exp2(b_slab[d:ts + d, :] - b0) for d in range(1, outer)]
    qcols, kcols = [], []
    for ja, jb in _level_slots(inner):
        jsel = jnp.where(lo, ja, jb)
        qcols.append(jnp.where(m == jsel, qd, 0.0))
        d = jsel - m
        f = jnp.where(d == 0, 1.0, 0.0)
        for dd in range(1, outer):
            f = jnp.where(d == dd, e[dd], f)
        kcols.append(kd * f)
    a_l = _dot_nt(jnp.concatenate(qcols, axis=1).astype(BF16), jnp.concatenate(kcols, axis=1).astype(BF16))
    return att + a_l * blk_s[len(_LEVELS) - 2]


def _dup_heads(x):
    out = []
    lo_half = _iota((x.shape[0], LANES), 1) < HEAD_DIM
    for pair in range(N_HEADS // 2):
        xp = x[:, pair * LANES:(pair + 1) * LANES]
        rot = pltpu.roll(xp, HEAD_DIM, 1)
        out.append(jnp.where(lo_half, xp, rot))
        out.append(jnp.where(lo_half, rot, xp))
    return out


def _hgrn2(z_c, lb, norm_g, state_t, b_slabs, tri_s, avg_s, blk_s, head_s):
    ts = z_c.shape[0]
    q, f, v, g = (z_c[:, n * W_GRP:(n + 1) * W_GRP] for n in range(4))
    qf = q * jax.nn.sigmoid(q)
    fgate = lb + (1.0 - lb) * jax.nn.sigmoid(f)
    log2f = jnp.log(fgate) * LOG2E
    kf = 1.0 - fgate
    tri = tri_s[...]
    p1, p2, p3 = _split3(log2f)
    bb = _dot(tri, p1) + _dot(tri, p2) + _dot(tri, p3)
    yield

    head = _iota((ts, W_GRP), 1) // HEAD_DIM
    v16 = v.astype(BF16)
    qh, kh = _dup_heads(qf), _dup_heads(kf)
    for h, bh in enumerate(_dup_heads(bb)):
        b_slabs[h, 0:ts, :] = bh
        b_slabs[h, ts:ts + SUBLANES, :] = jnp.zeros((SUBLANES, LANES), F32)
    yield
    o = _dot_nt((qf * jnp.exp2(bb)).astype(BF16), state_t[...].astype(BF16))
    for h in range(N_HEADS):
        att = _hgrn2_head_att(qh[h], kh[h], b_slabs.at[h], blk_s)
        o = o + _dot(att.astype(BF16), jnp.where(head == h, v16, jnp.zeros_like(v16)))
        yield

    bl = bb[ts - 1:ts, :]
    k_hat = (kf * jnp.exp2(bl - bb)).astype(BF16)
    upd = _dot_tn(v16, k_hat)
    state_t[...] = state_t[...] * jnp.exp2(bl) + upd * head_s[...]

    avg = avg_s[...]
    oo = o * o
    o_hi = oo.astype(BF16)
    o_lo = (oo - o_hi.astype(F32)).astype(BF16)
    ms = _dot(o_hi, avg) + _dot(o_lo, avg)
    o = o * lax.rsqrt(ms + EPS) * norm_g
    return o * (g * jax.nn.sigmoid(g))


def _pool(xd, dw, scale, pool_slab, tile_start):
    ts = xd.shape[0]
    lo_half = _iota((ts, LANES), 1) < HEAD_DIM
    pos = (tile_start + _iota((ts, LANES), 0) + 1).astype(F32)
    pooled = []
    for c in range(W_GRP // LANES):
        slab = pool_slab.at[c]
        x_tile = xd[:, c * LANES:(c + 1) * LANES]
        slab[POOL_HIST:POOL_HIST + ts, :] = x_tile
        w_lo, w_hi = POOL_WINDOWS[2 * c], POOL_WINDOWS[2 * c + 1]
        if w_lo % SUBLANES == 0 and w_hi == 2 * w_lo:
            acc = slab[POOL_HIST - w_lo:POOL_HIST + ts, :]
            for k in range(1, w_lo):
                acc = acc + slab[POOL_HIST - w_lo - k:POOL_HIST + ts - k, :]
            s_lo = acc[w_lo:]
            s_hi = s_lo + acc[:ts]
        else:
            acc = x_tile
            for k in range(1, w_lo):
                acc = acc + slab[POOL_HIST - k:POOL_HIST - k + ts, :]
            s_lo = acc
            for k in range(w_lo, w_hi):
                acc = acc + slab[POOL_HIST - k:POOL_HIST - k + ts, :]
            s_hi = acc
        slab[0:POOL_HIST, :] = slab[ts:ts + POOL_HIST, :]
        wsum = jnp.where(lo_half, s_lo, s_hi)
        wlen = jnp.where(lo_half, float(w_lo), float(w_hi))
        pooled.append(wsum / jnp.minimum(pos, wlen) - x_tile)
    return _dot(jnp.concatenate(pooled, axis=1).astype(BF16), dw) * scale


def _mixer_steps(layer, tile_start, x, consts, scratch, out):
    (g1_ref, win_ref, alng_ref, alnb_ref, aws_ref, absx_ref, bcw_ref, bcb_ref, bwax_ref, bbax_ref, blam_ref,
     clb_ref, cng_ref, dw_ref, dsc_ref, wout_ref) = consts
    conv_slab, a_slab, b_slab, h_carry, state_t, pool_slab, c_slabs, wm_s, tri_s, avg_s, blk_s, head_s = scratch
    h = _rms_norm(x, g1_ref[...]).astype(BF16)
    off_b, off_c, off_d = 2 * W_GRP, 4 * W_GRP, 8 * W_GRP

    y_a = yield from _gmlp(_dot(h, win_ref[:, 0:off_b]), alng_ref[...], alnb_ref[...], wm_s[...], absx_ref[...])
    yield
    y_b = yield from _rglru(_dot(h, win_ref[:, off_b:off_c]), bcw_ref[...], bcb_ref[...], bwax_ref[...], bbax_ref[...],
                 blam_ref[...], conv_slab, a_slab, b_slab, h_carry)
    yield
    clb = clb_ref[...]
    e = jnp.exp(clb - jnp.max(clb, axis=0, keepdims=True))
    sm = e / jnp.sum(e, axis=0, keepdims=True)
    lb = jnp.zeros((1, W_GRP), F32)
    for j in range(1, layer + 1):
        lb = lb + sm[j:j + 1, :]
    y_c = yield from _hgrn2(_dot(h, win_ref[:, off_c:off_d]), lb, cng_ref[...], state_t, c_slabs, tri_s, avg_s,
                            blk_s, head_s)
    yield
    y_d = _pool(_dot(h, win_ref[:, off_d:off_d + W_GRP]), dw_ref[...], dsc_ref[...], pool_slab, tile_start)
    yield
    mix = jnp.concatenate([y_a, y_b, y_c, y_d], axis=1).astype(BF16)
    out.append(x + _dot(mix, wout_ref[...]))


def _ffn_steps(final, x, p_tile, consts, hbuf, out):
    g2_ref, wup_ref, cw_ref, cb_ref, wdown_ref, g3_ref, wpe_ref, wpg_ref, gf_ref = consts
    d_ff = wdown_ref.shape[0]
    n_chunks = d_ff // FF_CHUNK
    h2 = _rms_norm(x, g2_ref[...]).astype(BF16)

    def up(col0):
        return _dot(h2, wup_ref[:, col0:col0 + FF_CHUNK])

    def conv(col0, hf):
        outs = []
        for c in range(FF_CHUNK // LANES):
            lanes = slice(col0 + c * LANES, col0 + (c + 1) * LANES)
            outs.append(_causal_conv_slab(hbuf.at[(col0 // LANES) + c], hf[:, c * LANES:(c + 1) * LANES],
                                          cw_ref[:, lanes], cb_ref[:, lanes]))
        return jnp.concatenate(outs, axis=1)

    acc = x
    hg_next, hv_next = up(0), up(d_ff)
    yield
    for j in range(n_chunks):
        hg, hv = hg_next, hv_next
        if j + 1 < n_chunks:
            hg_next, hv_next = up((j + 1) * FF_CHUNK), up(d_ff + (j + 1) * FF_CHUNK)
        gt = conv(j * FF_CHUNK, hg)
        val = conv(d_ff + j * FF_CHUNK, hv)
        act = (_gelu_tanh(gt) * val).astype(BF16)
        acc = acc + _dot(act, wdown_ref[j * FF_CHUNK:(j + 1) * FF_CHUNK, :])
        yield
    x = acc
    rstd = lax.rsqrt(jnp.mean(x * x, axis=-1, keepdims=True) + EPS)
    pe = _dot(p_tile.astype(BF16), wpe_ref[...])
    yield
    gate = jax.nn.sigmoid(_dot((x * rstd * g3_ref[...]).astype(BF16), wpg_ref[...]))
    x = x + pe * gate
    if final:
        x = _rms_norm(x, gf_ref[...])
    out.append(x)


N_MIXER_CONSTS = 16
N_FFN_CONSTS = 9
PHASE_ORDER = "fm" * 10 + "fmmfffm"


def _layer_kernel(layer, final, n_tiles, x_ref, p_ref, *refs):
    mixer_consts = refs[:N_MIXER_CONSTS]
    ffn_consts = refs[N_MIXER_CONSTS:N_MIXER_CONSTS + N_FFN_CONSTS]
    o_ref = refs[N_MIXER_CONSTS + N_FFN_CONSTS]
    (conv_slab, a_slab, b_slab, h_carry, state_t, pool_slab, c_slabs, wm_s, tri_s, avg_s, blk_s, head_s, hbuf,
     x1_buf) = refs[N_MIXER_CONSTS + N_FFN_CONSTS + 1:]
    i = pl.program_id(0)
    s = i % n_tiles

    @pl.when(i == 0)
    def _():
        x1_buf[...] = jnp.zeros_like(x1_buf)
        _init_constants(mixer_consts[4][...], wm_s, tri_s, avg_s, blk_s, head_s)

    @pl.when(s == 0)
    def _():
        conv_slab[:, 0:CONV_HIST, :] = jnp.zeros((conv_slab.shape[0], CONV_HIST, LANES), F32)
        pool_slab[:, 0:POOL_HIST, :] = jnp.zeros((pool_slab.shape[0], POOL_HIST, LANES), F32)
        h_carry[...] = jnp.zeros_like(h_carry)
        state_t[...] = jnp.zeros_like(state_t)

    @pl.when(jnp.maximum(i - 1, 0) % n_tiles == 0)
    def _():
        hbuf[:, 0:CONV_HIST, :] = jnp.zeros((hbuf.shape[0], CONV_HIST, LANES), F32)

    ts = x_ref.shape[0]
    mixer_out, ffn_out = [], []
    gens = [
        _ffn_steps(final, x1_buf[(i + 1) % 2], p_ref[...], ffn_consts, hbuf, ffn_out),
        _mixer_steps(layer, s * ts, x_ref[...], mixer_consts,
                     (conv_slab, a_slab, b_slab, h_carry, state_t, pool_slab, c_slabs, wm_s, tri_s, avg_s, blk_s,
                      head_s), mixer_out),
    ]
    ffn_gen, mixer_gen = gens
    phase = {"f": ffn_gen, "m": mixer_gen}
    for ch in PHASE_ORDER + "fm" * 4:
        try:
            next(phase[ch])
        except StopIteration:
            pass
    assert mixer_out and ffn_out
    x1_buf[i % 2] = mixer_out[0]
    o_ref[...] = ffn_out[0]


def _layer_spec(arr, layer):
    zeros = (0,) * (arr.ndim - 1)
    return pl.BlockSpec((None,) + arr.shape[1:], lambda i: (layer,) + zeros, pipeline_mode=pl.Buffered(1))


def _params():
    return pltpu.CompilerParams(dimension_semantics=("arbitrary",), vmem_limit_bytes=VMEM_LIMIT)


def _layer_call(layer, final, x, p, mixer_consts, ffn_consts):
    bsz, seq, d = x.shape
    n_tiles = seq // TS
    n_steps = bsz * n_tiles
    d_ff = ffn_consts[4].shape[1]
    consts = tuple(mixer_consts) + tuple(ffn_consts)
    assert len(mixer_consts) == N_MIXER_CONSTS and len(ffn_consts) == N_FFN_CONSTS

    def mixer_tile(i):
        t = jnp.minimum(i, n_steps - 1)
        return t // n_tiles, t % n_tiles

    def ffn_tile(i):
        t = jnp.maximum(i - 1, 0)
        return t // n_tiles, t % n_tiles

    return pl.pallas_call(
        functools.partial(_layer_kernel, layer, final, n_tiles),
        out_shape=jax.ShapeDtypeStruct(x.shape, x.dtype),
        grid=(n_steps + 1,),
        in_specs=[pl.BlockSpec((None, TS, d), lambda i: mixer_tile(i) + (0,)),
                  pl.BlockSpec((None, None, TS, p.shape[-1]), lambda i: (layer,) + ffn_tile(i) + (0,))]
        + [_layer_spec(c, layer) for c in consts],
        out_specs=pl.BlockSpec((None, TS, d), lambda i: ffn_tile(i) + (0,)),
        scratch_shapes=[
            pltpu.VMEM((W_GRP // LANES, CONV_HIST + TS, LANES), F32),
            pltpu.VMEM((W_GRP // LANES, SUBLANES * SCAN_PITCH, LANES), F32),
            pltpu.VMEM((W_GRP // LANES, SUBLANES * SCAN_PITCH, LANES), F32),
            pltpu.VMEM((SUBLANES, W_GRP), F32),
            pltpu.VMEM((W_GRP, W_GRP), F32),
            pltpu.VMEM((W_GRP // LANES, POOL_HIST + TS, LANES), F32),
            pltpu.VMEM((N_HEADS, TS + SUBLANES, LANES), F32),
            pltpu.VMEM((GMLP_CHUNK, N_HEADS * GMLP_CHUNK), BF16),
            pltpu.VMEM((TS, TS), BF16),
            pltpu.VMEM((W_GRP, W_GRP), BF16),
            pltpu.VMEM((len(_LEVELS) - 1, TS, TS), F32),
            pltpu.VMEM((W_GRP, W_GRP), F32),
            pltpu.VMEM((2 * d_ff // LANES, CONV_HIST + TS, LANES), F32),
            pltpu.VMEM((2, TS, d), F32),
        ],
        compiler_params=_params(),
        name=f"layer_l{layer}",
    )(x, p, *consts)


def _block_diag(w):
    depth, n_h, d_in, d_out = w.shape
    eye = jnp.eye(n_h, dtype=w.dtype)
    return jnp.einsum("lhde,hg->lhdge", w, eye).reshape(depth, n_h * d_in, n_h * d_out)


def _rows(v):
    return v.reshape(v.shape[0], 1, -1).astype(F32)


def kernel(x, p, norm1_g, w_in, a_ln_g, a_ln_b, a_ws, a_bs, b_conv_w, b_conv_b, b_wa, b_ba, b_wx, b_bx, b_lam, c_lb, c_norm_g, d_w, d_scale, w_out, norm2_g, w_up, ffn_conv_w, ffn_conv_b, w_down, norm3_g, w_pe, w_pg, final_g):
    depth = w_in.shape[0]
    d_ff = w_down.shape[1]
    assert x.shape[1] % TS == 0 and d_ff % FF_CHUNK == 0
    mixer_consts = (
        _rows(norm1_g),
        w_in.astype(BF16),
        _rows(a_ln_g), _rows(a_ln_b),
        a_ws.transpose(0, 2, 1, 3).reshape(depth, GMLP_CHUNK, N_HEADS * GMLP_CHUNK).astype(F32),
        jnp.repeat(a_bs.transpose(0, 2, 1), HEAD_DIM, axis=2).astype(F32),
        b_conv_w.astype(F32), _rows(b_conv_b),
        jnp.concatenate([_block_diag(b_wa), _block_diag(b_wx)], axis=2).astype(BF16),
        _rows(jnp.concatenate([b_ba, b_bx], axis=1)),
        _rows(b_lam),
        jnp.broadcast_to(c_lb.astype(F32), (depth,) + c_lb.shape),
        _rows(jnp.tile(c_norm_g, (1, N_HEADS))),
        _block_diag(d_w).astype(BF16), _rows(d_scale),
        w_out.astype(BF16),
    )
    ffn_consts = (
        _rows(norm2_g),
        w_up.astype(BF16),
        ffn_conv_w.astype(F32), _rows(ffn_conv_b),
        w_down.astype(BF16),
        _rows(norm3_g),
        w_pe.astype(BF16), w_pg.astype(BF16),
        jnp.broadcast_to(final_g.astype(F32).reshape(1, 1, -1), (depth, 1, final_g.shape[0])),
    )
    for l in range(depth):
        x = _layer_call(l, l == depth - 1, x, p, mixer_consts, ffn_consts)
    return x
```
